```python
import jax, jax.numpy as jnp
from jax import lax
import numpy as np

D_MODEL = 1024
BATCH = 8
SEQ = 2048
DEPTH = 4

ATTN_WIDTH = D_MODEL // 2
RET_WIDTH = D_MODEL - ATTN_WIDTH
HEAD_DIM = 64
N_ATTN_HEADS = ATTN_WIDTH // HEAD_DIM
N_KV_HEADS = 2
GQA_GROUP = N_ATTN_HEADS // N_KV_HEADS
KV_WIDTH = N_KV_HEADS * HEAD_DIM
WINDOW = 128
ATTN_BLOCK = 128
N_RET_HEADS = 4
RET_HEAD_DIM = RET_WIDTH // N_RET_HEADS
RET_CHUNK = 128
ROPE_BASE = 10000.0
D_FF = 2816
NORM_EPS = 1e-6
GN_EPS = 1e-5
NEG_INF = -1e30
IN_WIDTHS = (ATTN_WIDTH, KV_WIDTH, KV_WIDTH, RET_WIDTH, RET_WIDTH, RET_WIDTH, RET_WIDTH)
IN_SPLITS = tuple(int(v) for v in np.cumsum(IN_WIDTHS)[:-1])
D_IN = int(sum(IN_WIDTHS))

kernel_name = "hymba_style_swa_sink_retention_macaron"


def rms_norm(x, w):
    xf = x.astype(jnp.float32)
    y = xf * lax.rsqrt(jnp.mean(xf * xf, axis=-1, keepdims=True) + NORM_EPS)
    return (y * w.astype(jnp.float32)).astype(x.dtype)


def swiglu(h, w_gate, w_up, w_down):
    return (jax.nn.silu(h @ w_gate) * (h @ w_up)) @ w_down


def sliding_window_sink_attention(q, k, v, sinks):
    B, S, _ = q.shape
    nb = S // ATTN_BLOCK
    q = q.reshape(B, nb, ATTN_BLOCK, N_KV_HEADS, GQA_GROUP, HEAD_DIM)
    k = k.reshape(B, nb, ATTN_BLOCK, N_KV_HEADS, HEAD_DIM)
    v = v.reshape(B, nb, ATTN_BLOCK, N_KV_HEADS, HEAD_DIM)
    pad = ((0, 0), (1, 0), (0, 0), (0, 0), (0, 0))
    kk = jnp.concatenate([jnp.pad(k[:, :-1], pad), k], axis=2)
    vv = jnp.concatenate([jnp.pad(v[:, :-1], pad), v], axis=2)
    s = jnp.einsum('bnqhgd,bnkhd->bnhgqk', q, kk).astype(jnp.float32) * (HEAD_DIM ** -0.5)
    blk = jnp.arange(nb)[:, None, None]
    qi = jnp.arange(ATTN_BLOCK)[None, :, None]
    kj = jnp.arange(2 * ATTN_BLOCK)[None, None, :]
    diff = ATTN_BLOCK + qi - kj
    kpos = (blk - 1) * ATTN_BLOCK + kj
    mask = (diff >= 0) & (diff < WINDOW) & (kpos >= 0)
    s = jnp.where(mask[None, :, None, None], s, NEG_INF)
    sink = sinks.astype(jnp.float32).reshape(1, 1, N_KV_HEADS, GQA_GROUP, 1, 1)
    m = jnp.maximum(jnp.max(s, axis=-1, keepdims=True), sink)
    p = jnp.exp(s - m)
    denom = jnp.sum(p, axis=-1, keepdims=True) + jnp.exp(sink - m)
    probs = (p / denom).astype(v.dtype)
    out = jnp.einsum('bnhgqk,bnkhd->bnqhgd', probs, vv)
    return out.reshape(B, S, ATTN_WIDTH)


def rotary(x, cos, sin):
    half = x.shape[-1] // 2
    x1, x2 = x[..., :half], x[..., half:]
    c = cos[None, :, None, :].astype(x.dtype)
    s = sin[None, :, None, :].astype(x.dtype)
    return jnp.concatenate([x1 * c - x2 * s, x1 * s + x2 * c], axis=-1)


def multiscale_retention(q, k, v, g, gn_w):
    B, S, _ = q.shape
    H, D, C = N_RET_HEADS, RET_HEAD_DIM, RET_CHUNK
    nc = S // C
    pos = jnp.arange(S, dtype=jnp.float32)
    inv_freq = ROPE_BASE ** (-jnp.arange(0, D, 2, dtype=jnp.float32) / D)
    ang = pos[:, None] * inv_freq[None, :]
    cos, sin = jnp.cos(ang), jnp.sin(ang)
    q = rotary(q.reshape(B, S, H, D), cos, sin)
    k = rotary(k.reshape(B, S, H, D), cos, sin) * (D ** -0.5)
    v = v.reshape(B, S, H, D)
    q = q.reshape(B, nc, C, H, D)
    k = k.reshape(B, nc, C, H, D)
    v = v.reshape(B, nc, C, H, D)
    log_gamma = jnp.log(1.0 - 2.0 ** (-5.0 - jnp.arange(H, dtype=jnp.float32)))
    idx = jnp.arange(C, dtype=jnp.float32)
    dif = idx[:, None] - idx[None, :]
    dmat = jnp.where(dif[None] >= 0, jnp.exp(jnp.maximum(dif, 0.0)[None] * log_gamma[:, None, None]), 0.0)
    zeta = jnp.exp((C - 1.0 - idx)[None, :] * log_gamma[:, None])
    xi = jnp.exp((idx + 1.0)[None, :] * log_gamma[:, None])
    chunk_decay = jnp.exp(C * log_gamma).astype(q.dtype)
    scores = jnp.einsum('bnihd,bnjhd->bnhij', q, k) * dmat.astype(q.dtype)[None, None]
    y_intra = jnp.einsum('bnhij,bnjhv->bnihv', scores, v)
    kv = jnp.einsum('bnjhd,bnjhv,hj->bnhdv', k, v, zeta.astype(q.dtype))

    def step(state, kv_n):
        return state * chunk_decay[None, :, None, None] + kv_n, state

    init = jnp.zeros((B, H, D, D), dtype=kv.dtype)
    _, prev = lax.scan(step, init, jnp.moveaxis(kv, 1, 0))
    prev = jnp.moveaxis(prev, 0, 1)
    y_cross = jnp.einsum('bnihd,bnhdv->bnihv', q, prev) * xi.T.astype(q.dtype)[None, None, :, :, None]
    y = (y_intra + y_cross).reshape(B, S, H, D)
    yf = y.astype(jnp.float32)
    mu = jnp.mean(yf, axis=-1, keepdims=True)
    var = jnp.mean(jnp.square(yf - mu), axis=-1, keepdims=True)
    yn = ((yf - mu) * lax.rsqrt(var + GN_EPS)).reshape(B, S, RET_WIDTH)
    yn = (yn * gn_w.astype(jnp.float32)).astype(q.dtype)
    return jax.nn.silu(g) * yn


def setup_inputs(seed: int = 0) -> dict:
    key = jax.random.key(seed)
    ks = jax.random.split(key, 16)

    def w(k, shape, fan_in):
        return jax.random.normal(k, shape, jnp.float32) * fan_in ** -0.5

    def gain(k, shape):
        return 1.0 + 0.02 * jax.random.normal(k, shape, jnp.float32)

    return {
        "x": jax.random.normal(ks[0], (BATCH, SEQ, D_MODEL), jnp.float32),
        "ffn1_norm": gain(ks[1], (DEPTH, D_MODEL)),
        "ffn1_w_gate": w(ks[2], (DEPTH, D_MODEL, D_FF), D_MODEL),
        "ffn1_w_up": w(ks[3], (DEPTH, D_MODEL, D_FF), D_MODEL),
        "ffn1_w_down": w(ks[4], (DEPTH, D_FF, D_MODEL), D_FF),
        "mix_norm": gain(ks[5], (DEPTH, D_MODEL)),
        "w_in": w(ks[6], (DEPTH, D_MODEL, D_IN), D_MODEL),
        "attn_sinks": 0.5 * jax.random.normal(ks[7], (DEPTH, N_ATTN_HEADS), jnp.float32),
        "ret_gn_w": gain(ks[8], (DEPTH, RET_WIDTH)),
        "w_out": w(ks[9], (DEPTH, D_MODEL, D_MODEL), D_MODEL),
        "ffn2_norm": gain(ks[10], (DEPTH, D_MODEL)),
        "ffn2_w_gate": w(ks[11], (DEPTH, D_MODEL, D_FF), D_MODEL),
        "ffn2_w_up": w(ks[12], (DEPTH, D_MODEL, D_FF), D_MODEL),
        "ffn2_w_down": w(ks[13], (DEPTH, D_FF, D_MODEL), D_FF),
        "final_norm": gain(ks[14], (D_MODEL,)),
    }


def reference(x, ffn1_norm, ffn1_w_gate, ffn1_w_up, ffn1_w_down, mix_norm, w_in,
              attn_sinks, ret_gn_w, w_out, ffn2_norm, ffn2_w_gate, ffn2_w_up,
              ffn2_w_down, final_norm):
    h = x
    for l in range(DEPTH):
        h = h + 0.5 * swiglu(rms_norm(h, ffn1_norm[l]), ffn1_w_gate[l], ffn1_w_up[l], ffn1_w_down[l])
        u = rms_norm(h, mix_norm[l])
        aq, ak, av, rq, rk, rv, rg = jnp.split(u @ w_in[l], IN_SPLITS, axis=-1)
        a = sliding_window_sink_attention(aq, ak, av, attn_sinks[l])
        r = multiscale_retention(rq, rk, rv, rg, ret_gn_w[l])
        h = h + jnp.concatenate([a, r], axis=-1) @ w_out[l]
        h = h + 0.5 * swiglu(rms_norm(h, ffn2_norm[l]), ffn2_w_gate[l], ffn2_w_up[l], ffn2_w_down[l])
    return rms_norm(h, final_norm)
```

```python
import functools

import jax
import jax.numpy as jnp
from jax import lax
from jax.experimental import pallas as pl
from jax.experimental.pallas import tpu as pltpu

D_MODEL = 1024
BATCH = 8
SEQ = 2048
DEPTH = 4
ATTN_WIDTH = 512
RET_WIDTH = 512
HEAD_DIM = 64
N_ATTN_HEADS = 8
N_KV_HEADS = 2
GQA_GROUP = 4
KV_WIDTH = 128
WINDOW = 128
BLK = 128
N_RET_HEADS = 4
RET_HEAD_DIM = 128
ROPE_BASE = 10000.0
D_FF = 2816
NORM_EPS = 1e-6
GN_EPS = 1e-5
NEG_INF = -1e30
N_TOK = BATCH * SEQ

C_AQ, C_KD, C_VD, C_RQ, C_RK, C_RV, C_RG, D_INW = 0, 512, 768, 1024, 1536, 2048, 2560, 3072

TM = 512
TQ = 256
FF_SPLITS = ((0, 1536), (1536, 2816))
VMEM_LIMIT = 56 * 1024 * 1024

F32 = jnp.float32
BF16 = jnp.bfloat16


def _dot(a, b):
    return jnp.dot(a, b, preferred_element_type=F32)


def _dot_nt(a, b):
    return lax.dot_general(a, b, (((1,), (1,)), ((), ())), preferred_element_type=F32)


def _dot_tn(a, b):
    return lax.dot_general(a, b, (((0,), (0,)), ((), ())), preferred_element_type=F32)


def _rms(x, w):
    y = x * lax.rsqrt(jnp.mean(x * x, axis=-1, keepdims=True) + NORM_EPS)
    return y * w


def _swiglu(xn, wg_ref, wu_ref, wd_ref, hmid_ref):
    for lo, hi in FF_SPLITS:
        g = _dot(xn, wg_ref[:, lo:hi])
        u = _dot(xn, wu_ref[:, lo:hi])
        hmid_ref[:, lo:hi] = (jax.nn.silu(g) * u).astype(BF16)
    return _dot(hmid_ref[...], wd_ref[...])


def _pre_kernel(x_ref, n1_ref, wg_ref, wu_ref, wd_ref, n2_ref, win_ref, cos_ref, sin_ref,
                h_ref, qa_ref, kd_ref, vd_ref, rq_ref, rk_ref, rv_ref, sg_ref, hmid_ref):
    x = x_ref[...]
    xn = _rms(x, n1_ref[...]).astype(BF16)
    h = x + 0.5 * _swiglu(xn, wg_ref, wu_ref, wd_ref, hmid_ref)
    h_ref[...] = h
    u = _rms(h, n2_ref[...]).astype(BF16)

    def proj(lo, hi):
        return _dot(u, win_ref[:, lo:hi])

    qa_ref[...] = (proj(C_AQ, C_KD) * (HEAD_DIM ** -0.5)).astype(BF16)
    kd_ref[...] = proj(C_KD, C_VD).astype(BF16)
    vd_ref[...] = proj(C_VD, C_RQ).astype(BF16)
    cos = cos_ref[...]
    sin = sin_ref[...]
    q = proj(C_RQ, C_RK)
    k = proj(C_RK, C_RV)
    for hd in range(N_RET_HEADS):
        sl = slice(hd * RET_HEAD_DIM, (hd + 1) * RET_HEAD_DIM)
        qh = q[:, sl]
        kh = k[:, sl]
        rq_ref[:, sl] = (qh * cos + pltpu.roll(qh, RET_HEAD_DIM // 2, 1) * sin).astype(BF16)
        kr = kh * cos + pltpu.roll(kh, RET_HEAD_DIM // 2, 1) * sin
        rk_ref[:, sl] = (kr * (RET_HEAD_DIM ** -0.5)).astype(BF16)
    rv_ref[...] = proj(C_RV, C_RG).astype(BF16)
    sg_ref[...] = jax.nn.silu(proj(C_RG, D_INW)).astype(BF16)


def _post_kernel(final, h_ref, ar_ref, wo_ref, n_ref, wg_ref, wu_ref, wd_ref, fn_ref, o_ref, hmid_ref):
    h2 = h_ref[...] + _dot(ar_ref[...], wo_ref[...])
    xn = _rms(h2, n_ref[...]).astype(BF16)
    y = h2 + 0.5 * _swiglu(xn, wg_ref, wu_ref, wd_ref, hmid_ref)
    if final:
        y = _rms(y, fn_ref[...])
    o_ref[...] = y


def _mix_kernel(sink_ref, decay_ref, qa_ref, kd_ref, kdp_ref, vd_ref, vdp_ref, rq_ref, rk_ref, rv_ref,
                sg_ref, dmat_ref, zeta_ref, xi_ref, gnw_ref, o_ref, state_ref):
    i = pl.program_id(1)

    @pl.when(i == 0)
    def _():
        state_ref[...] = jnp.zeros_like(state_ref)

    row = lax.broadcasted_iota(jnp.int32, (BLK, 2 * BLK), 0)
    col = lax.broadcasted_iota(jnp.int32, (BLK, 2 * BLK), 1)
    band = (col > row) & (col <= row + WINDOW)
    band0 = band & ((col >= BLK) | (i > 0))
    lo = lax.broadcasted_iota(jnp.int32, (BLK, 2 * HEAD_DIM), 1) < HEAD_DIM
    zero_bf = jnp.zeros((BLK, 2 * HEAD_DIM), BF16)

    for j in range(TQ // BLK):
        r0 = j * BLK
        rows = slice(r0, r0 + BLK)
        if j == 0:
            kk = jnp.concatenate([kdp_ref[...], kd_ref[0:BLK, :]], axis=0)
            vv = jnp.concatenate([vdp_ref[...], vd_ref[0:BLK, :]], axis=0)
            valid = band0
        else:
            kk = kd_ref[r0 - BLK:r0 + BLK, :]
            vv = vd_ref[r0 - BLK:r0 + BLK, :]
            valid = band

        for kvh in range(N_KV_HEADS):
            ksl = slice(kvh * 2 * HEAD_DIM, (kvh + 1) * 2 * HEAD_DIM)
            k_h = kk[:, ksl]
            v_h = vv[:, ksl]
            qs = []
            for p in range(GQA_GROUP // 2):
                c0 = (kvh * 2 + p) * 2 * HEAD_DIM
                qp = qa_ref[rows, c0:c0 + 2 * HEAD_DIM]
                qs.append(jnp.where(lo, qp, zero_bf))
                qs.append(jnp.where(lo, zero_bf, qp))
            s_all = _dot_nt(jnp.concatenate(qs, axis=0), k_h)
            ps, invs = [], []
            for g in range(GQA_GROUP):
                sink = sink_ref[kvh * GQA_GROUP + g]
                s = jnp.where(valid, s_all[g * BLK:(g + 1) * BLK, :], NEG_INF)
                m = jnp.maximum(jnp.max(s, axis=-1, keepdims=True), sink)
                e = jnp.exp(s - m)
                den = jnp.sum(e, axis=-1, keepdims=True) + jnp.exp(sink - m)
                ps.append(e.astype(BF16))
                invs.append(1.0 / den)
            o_all = _dot(jnp.concatenate(ps, axis=0), v_h)
            for p in range(GQA_GROUP // 2):
                c0 = (kvh * 2 + p) * 2 * HEAD_DIM
                oe = o_all[(2 * p) * BLK:(2 * p + 1) * BLK, :] * invs[2 * p]
                oo = o_all[(2 * p + 1) * BLK:(2 * p + 2) * BLK, :] * invs[2 * p + 1]
                o_ref[rows, c0:c0 + 2 * HEAD_DIM] = jnp.where(lo, oe, oo).astype(BF16)

        for hd in range(N_RET_HEADS):
            sl = slice(hd * RET_HEAD_DIM, (hd + 1) * RET_HEAD_DIM)
            q = rq_ref[rows, sl]
            k = rk_ref[rows, sl]
            v = rv_ref[rows, sl]
            st = state_ref[hd]
            sc = _dot_nt(q, k) * dmat_ref[hd]
            y = _dot(sc.astype(BF16), v) + _dot(q, st.astype(BF16)) * xi_ref[hd]
            vz = (v.astype(F32) * zeta_ref[hd]).astype(BF16)
            state_ref[hd] = st * decay_ref[hd] + _dot_tn(k, vz)
            mu = jnp.mean(y, axis=-1, keepdims=True)
            d = y - mu
            var = jnp.mean(d * d, axis=-1, keepdims=True)
            yn = d * lax.rsqrt(var + GN_EPS) * gnw_ref[:, sl]
            o_ref[rows, ATTN_WIDTH + hd * RET_HEAD_DIM:ATTN_WIDTH + (hd + 1) * RET_HEAD_DIM] = (
                sg_ref[rows, sl].astype(F32) * yn).astype(BF16)


def _resident(shape, layer):
    nd = len(shape)
    return pl.BlockSpec((None,) + shape, lambda *_: (layer,) + (0,) * nd, pipeline_mode=pl.Buffered(1))


def _const(shape):
    nd = len(shape)
    return pl.BlockSpec(shape, lambda *_: (0,) * nd, pipeline_mode=pl.Buffered(1))


def _tok(width):
    return pl.BlockSpec((TM, width), lambda i: (i, 0))


def _pre_call(layer, x, n1, wg, wu, wd, n2, win, cos_t, sin_t):
    n_seq_tiles = SEQ // TM
    outs = [jax.ShapeDtypeStruct((N_TOK, D_MODEL), F32)]
    outs += [jax.ShapeDtypeStruct((N_TOK, w), BF16) for w in (512, 256, 256, 512, 512, 512, 512)]
    return pl.pallas_call(
        _pre_kernel,
        grid=(N_TOK // TM,),
        in_specs=[
            _tok(D_MODEL),
            _resident((1, D_MODEL), layer),
            _resident((D_MODEL, D_FF), layer),
            _resident((D_MODEL, D_FF), layer),
            _resident((D_FF, D_MODEL), layer),
            _resident((1, D_MODEL), layer),
            _resident((D_MODEL, D_INW), layer),
            pl.BlockSpec((TM, RET_HEAD_DIM), lambda i: (i % n_seq_tiles, 0)),
            pl.BlockSpec((TM, RET_HEAD_DIM), lambda i: (i % n_seq_tiles, 0)),
        ],
        out_specs=[_tok(D_MODEL)] + [_tok(w) for w in (512, 256, 256, 512, 512, 512, 512)],
        out_shape=outs,
        scratch_shapes=[pltpu.VMEM((TM, D_FF), BF16)],
        compiler_params=pltpu.CompilerParams(dimension_semantics=("arbitrary",), vmem_limit_bytes=VMEM_LIMIT),
        name=f"pre{layer}",
    )(x, n1, wg, wu, wd, n2, win, cos_t, sin_t)


def _post_call(layer, final, h, ar, wo, n, wg, wu, wd, fn):
    return pl.pallas_call(
        functools.partial(_post_kernel, final),
        grid=(N_TOK // TM,),
        in_specs=[
            _tok(D_MODEL),
            _tok(D_MODEL),
            _resident((D_MODEL, D_MODEL), layer),
            _resident((1, D_MODEL), layer),
            _resident((D_MODEL, D_FF), layer),
            _resident((D_MODEL, D_FF), layer),
            _resident((D_FF, D_MODEL), layer),
            _const((1, D_MODEL)),
        ],
        out_specs=_tok(D_MODEL),
        out_shape=jax.ShapeDtypeStruct((N_TOK, D_MODEL), F32),
        scratch_shapes=[pltpu.VMEM((TM, D_FF), BF16)],
        compiler_params=pltpu.CompilerParams(dimension_semantics=("arbitrary",), vmem_limit_bytes=VMEM_LIMIT),
        name=f"post{layer}",
    )(h, ar, wo, n, wg, wu, wd, fn)


def _mix_call(layer, sinks, decay, qa, kd, vd, rq, rk, rv, sg, dmat, zeta_b, xi_b, gnw):
    nt = SEQ // TQ
    nb = TQ // BLK

    def cur(width):
        return pl.BlockSpec((TQ, width), lambda b, i: (b * nt + i, 0))

    def prev(width):
        return pl.BlockSpec((BLK, width), lambda b, i: (b * (SEQ // BLK) + jnp.maximum(i * nb - 1, 0), 0))

    smem = pl.BlockSpec(memory_space=pltpu.SMEM)
    tab = pl.BlockSpec((N_RET_HEADS, BLK, BLK), lambda b, i: (0, 0, 0))
    return pl.pallas_call(
        _mix_kernel,
        grid=(BATCH, nt),
        in_specs=[smem, smem, cur(512), cur(256), prev(256), cur(256), prev(256),
                  cur(512), cur(512), cur(512), cur(512), tab, tab, tab,
                  pl.BlockSpec((None, 1, RET_WIDTH), lambda b, i: (layer, 0, 0))],
        out_specs=cur(D_MODEL),
        out_shape=jax.ShapeDtypeStruct((N_TOK, D_MODEL), BF16),
        scratch_shapes=[pltpu.VMEM((N_RET_HEADS, RET_HEAD_DIM, RET_HEAD_DIM), F32)],
        compiler_params=pltpu.CompilerParams(dimension_semantics=("arbitrary", "arbitrary"),
                                             vmem_limit_bytes=VMEM_LIMIT),
        name=f"mix{layer}",
    )(sinks, decay, qa, kd, kd, vd, vd, rq, rk, rv, sg, dmat, zeta_b, xi_b, gnw)


def _tables():
    pos = jnp.arange(SEQ, dtype=F32)
    inv_freq = ROPE_BASE ** (-jnp.arange(0, RET_HEAD_DIM, 2, dtype=F32) / RET_HEAD_DIM)
    ang = pos[:, None] * inv_freq[None, :]
    cos, sin = jnp.cos(ang), jnp.sin(ang)
    cos_t = jnp.concatenate([cos, cos], axis=-1)
    sin_t = jnp.concatenate([-sin, sin], axis=-1)
    log_gamma = jnp.log(1.0 - 2.0 ** (-5.0 - jnp.arange(N_RET_HEADS, dtype=F32)))
    idx = jnp.arange(BLK, dtype=F32)
    dif = idx[:, None] - idx[None, :]
    dmat = jnp.where(dif[None] >= 0, jnp.exp(jnp.maximum(dif, 0.0)[None] * log_gamma[:, None, None]), 0.0)
    zeta = jnp.exp((BLK - 1.0 - idx)[None, :] * log_gamma[:, None])
    xi = jnp.exp((idx + 1.0)[None, :] * log_gamma[:, None])
    decay = jnp.exp(BLK * log_gamma)
    bshape = (N_RET_HEADS, BLK, RET_HEAD_DIM)
    zeta_b = jnp.broadcast_to(zeta[:, :, None], bshape)
    xi_b = jnp.broadcast_to(xi[:, :, None], bshape)
    return cos_t, sin_t, dmat, zeta_b, xi_b, decay


def kernel(x, ffn1_norm, ffn1_w_gate, ffn1_w_up, ffn1_w_down, mix_norm, w_in, attn_sinks, ret_gn_w, w_out,
           ffn2_norm, ffn2_w_gate, ffn2_w_up, ffn2_w_down, final_norm):
    cos_t, sin_t, dmat, zeta_b, xi_b, decay = _tables()
    hd = HEAD_DIM
    k0, k1 = w_in[..., 512:512 + hd], w_in[..., 512 + hd:640]
    v0, v1 = w_in[..., 640:640 + hd], w_in[..., 640 + hd:768]
    win = jnp.concatenate([w_in[..., :512], k0, k0, k1, k1, v0, v0, v1, v1, w_in[..., 768:]], axis=-1).astype(BF16)
    wg1, wu1, wd1 = ffn1_w_gate.astype(BF16), ffn1_w_up.astype(BF16), ffn1_w_down.astype(BF16)
    wg2, wu2, wd2 = ffn2_w_gate.astype(BF16), ffn2_w_up.astype(BF16), ffn2_w_down.astype(BF16)
    wo = w_out.astype(BF16)
    n1 = ffn1_norm.reshape(DEPTH, 1, D_MODEL)
    nm = mix_norm.reshape(DEPTH, 1, D_MODEL)
    n2 = ffn2_norm.reshape(DEPTH, 1, D_MODEL)
    gnw = ret_gn_w.reshape(DEPTH, 1, RET_WIDTH)
    fn = final_norm.reshape(1, D_MODEL)

    h = x.reshape(N_TOK, D_MODEL)
    for layer in range(DEPTH):
        h, qa, kd, vd, rq, rk, rv, sg = _pre_call(layer, h, n1, wg1, wu1, wd1, nm, win, cos_t, sin_t)
        ar = _mix_call(layer, attn_sinks[layer], decay, qa, kd, vd, rq, rk, rv, sg, dmat, zeta_b, xi_b, gnw)
        h = _post_call(layer, layer == DEPTH - 1, h, ar, wo, n2, wg2, wu2, wd2, fn)
    return h.reshape(BATCH, SEQ, D_MODEL)
```

```python
import functools

import jax
import jax.numpy as jnp
from jax import lax
from jax.experimental import pallas as pl
from jax.experimental.pallas import tpu as pltpu

D_MODEL = 1024
BATCH = 8
SEQ = 2048
DEPTH = 4
ATTN_WIDTH = 512
RET_WIDTH = 512
HEAD_DIM = 64
N_KV_HEADS = 2
GQA_GROUP = 4
KV_WIDTH = 128
WINDOW = 128
BLK = 128
N_RET_HEADS = 4
RET_HEAD_DIM = 128
ROPE_BASE = 10000.0
D_FF = 2816
D_IN = 2816
NORM_EPS = 1e-6
GN_EPS = 1e-5
NEG_INF = -1e30
N_TOK = BATCH * SEQ

W_AQ, W_KV, W_RQ, W_RK, W_RV, W_RG = 0, 512, 768, 1280, 1792, 2304
A_AQ, A_KD, A_VD, A_RQ, A_RK, A_RV, A_SG, D_ACT = 0, 512, 768, 1024, 1536, 2048, 2560, 3072

TM = 512
N_TILES = N_TOK // TM
TILES_PER_SEQ = SEQ // TM
BLKS_PER_TILE = TM // BLK
FF_SPLITS = ((0, 1536), (1536, 2816))
FF_CHUNK = 256
OUT_CHUNK = 256
VMEM_LIMIT = 56 * 1024 * 1024

F32 = jnp.float32
BF16 = jnp.bfloat16


def _dot(a, b):
    return jnp.dot(a, b, preferred_element_type=F32)


def _dot_nt(a, b):
    return lax.dot_general(a, b, (((1,), (1,)), ((), ())), preferred_element_type=F32)


def _dot_tn(a, b):
    return lax.dot_general(a, b, (((0,), (0,)), ((), ())), preferred_element_type=F32)


def _rms(x, w):
    y = x * lax.rsqrt(jnp.mean(x * x, axis=-1, keepdims=True) + NORM_EPS)
    return y * w


def _swap_halves(x):
    return pltpu.roll(x, x.shape[-1] // 2, 1)


def _pre_kernel(x_ref, n1_ref, wg_ref, wu_ref, wd_ref, n2_ref, win_ref, cos_ref, sin_ref,
                h_ref, act_ref, hmid_ref):
    x = x_ref[...]
    xn = _rms(x, n1_ref[...]).astype(BF16)
    for lo, hi in FF_SPLITS:
        g = _dot(xn, wg_ref[:, lo:hi])
        u = _dot(xn, wu_ref[:, lo:hi])
        hmid_ref[:, lo:hi] = (jax.nn.silu(g) * u).astype(BF16)
    h = x + 0.5 * _dot(hmid_ref[...], wd_ref[...])
    h_ref[...] = h
    u = _rms(h, n2_ref[...]).astype(BF16)

    def proj(lo, hi):
        return _dot(u, win_ref[:, lo:hi])

    act_ref[:, A_AQ:A_KD] = (proj(W_AQ, W_KV) * (HEAD_DIM ** -0.5)).astype(BF16)

    kv = proj(W_KV, W_RQ)
    first = lax.broadcasted_iota(jnp.int32, (TM, KV_WIDTH), 1) < HEAD_DIM
    for src, dst in ((0, A_KD), (KV_WIDTH, A_VD)):
        t = kv[:, src:src + KV_WIDTH]
        ts = _swap_halves(t)
        act_ref[:, dst:dst + KV_WIDTH] = jnp.where(first, t, ts).astype(BF16)
        act_ref[:, dst + KV_WIDTH:dst + 2 * KV_WIDTH] = jnp.where(first, ts, t).astype(BF16)

    pos0 = pl.multiple_of((pl.program_id(0) % TILES_PER_SEQ) * TM, TM)
    cos = cos_ref[pl.ds(pos0, TM), :]
    sin = sin_ref[pl.ds(pos0, TM), :]
    q = proj(W_RQ, W_RK)
    k = proj(W_RK, W_RV)
    for hd in range(N_RET_HEADS):
        sl = slice(hd * RET_HEAD_DIM, (hd + 1) * RET_HEAD_DIM)
        qh = q[:, sl]
        kh = k[:, sl]
        act_ref[:, A_RQ + sl.start:A_RQ + sl.stop] = (qh * cos + _swap_halves(qh) * sin).astype(BF16)
        kr = kh * cos + _swap_halves(kh) * sin
        act_ref[:, A_RK + sl.start:A_RK + sl.stop] = (kr * (RET_HEAD_DIM ** -0.5)).astype(BF16)
    act_ref[:, A_RV:A_SG] = proj(W_RV, W_RG).astype(BF16)
    act_ref[:, A_SG:D_ACT] = jax.nn.silu(proj(W_RG, D_IN)).astype(BF16)


def _mix_units(seq_start, sink_ref, decay_ref, act_ref, kvp_ref, dmat_ref, zeta_ref, xi_ref, gnw_ref,
               o_ref, state_ref):
    row = lax.broadcasted_iota(jnp.int32, (BLK, 2 * BLK), 0)
    col = lax.broadcasted_iota(jnp.int32, (BLK, 2 * BLK), 1)
    band = (col > row) & (col <= row + WINDOW)
    band0 = band & ((col >= BLK) | jnp.logical_not(seq_start))
    lo = lax.broadcasted_iota(jnp.int32, (BLK, 2 * HEAD_DIM), 1) < HEAD_DIM
    zero_bf = jnp.zeros((BLK, 2 * HEAD_DIM), BF16)
    pair_w = 2 * HEAD_DIM

    for j in range(BLKS_PER_TILE):
        r0 = j * BLK
        rows = slice(r0, r0 + BLK)
        valid = band0 if j == 0 else band

        for kvh in range(N_KV_HEADS):
            kc = A_KD + kvh * pair_w
            vc = A_VD + kvh * pair_w
            if j == 0:
                k_h = jnp.concatenate([kvp_ref[:, kc - A_KD:kc - A_KD + pair_w], act_ref[0:BLK, kc:kc + pair_w]], axis=0)
                v_h = jnp.concatenate([kvp_ref[:, vc - A_KD:vc - A_KD + pair_w], act_ref[0:BLK, vc:vc + pair_w]], axis=0)
            else:
                k_h = act_ref[r0 - BLK:r0 + BLK, kc:kc + pair_w]
                v_h = act_ref[r0 - BLK:r0 + BLK, vc:vc + pair_w]
            qs = []
            for p in range(GQA_GROUP // 2):
                c0 = A_AQ + (kvh * 2 + p) * pair_w
                qp = act_ref[rows, c0:c0 + pair_w]
                qs.append(jnp.where(lo, qp, zero_bf))
                qs.append(jnp.where(lo, zero_bf, qp))
            s_all = _dot_nt(jnp.concatenate(qs, axis=0), k_h)
            ps, invs = [], []
            for g in range(GQA_GROUP):
                sink = sink_ref[kvh * GQA_GROUP + g]
                s = jnp.where(valid, s_all[g * BLK:(g + 1) * BLK, :], NEG_INF)
                m = jnp.maximum(jnp.max(s, axis=-1, keepdims=True), sink)
                e = jnp.exp(s - m)
                den = jnp.sum(e, axis=-1, keepdims=True) + jnp.exp(sink - m)
                ps.append(e.astype(BF16))
                invs.append(1.0 / den)
            o_all = _dot(jnp.concatenate(ps, axis=0), v_h)
            for p in range(GQA_GROUP // 2):
                c0 = (kvh * 2 + p) * pair_w
                oe = o_all[(2 * p) * BLK:(2 * p + 1) * BLK, :] * invs[2 * p]
                oo = o_all[(2 * p + 1) * BLK:(2 * p + 2) * BLK, :] * invs[2 * p + 1]
                o_ref[rows, c0:c0 + pair_w] = jnp.where(lo, oe, oo).astype(BF16)
            yield

        for hd in range(N_RET_HEADS):
            c = hd * RET_HEAD_DIM
            q = act_ref[rows, A_RQ + c:A_RQ + c + RET_HEAD_DIM]
            k = act_ref[rows, A_RK + c:A_RK + c + RET_HEAD_DIM]
            v = act_ref[rows, A_RV + c:A_RV + c + RET_HEAD_DIM]
            sg = act_ref[rows, A_SG + c:A_SG + c + RET_HEAD_DIM]
            st = state_ref[hd]
            sc = _dot_nt(q, k) * dmat_ref[hd]
            y = _dot(sc.astype(BF16), v) + _dot(q, st.astype(BF16)) * xi_ref[hd]
            vz = (v.astype(F32) * zeta_ref[hd]).astype(BF16)
            state_ref[hd] = st * decay_ref[hd] + _dot_tn(k, vz)
            mu = jnp.mean(y, axis=-1, keepdims=True)
            d = y - mu
            var = jnp.mean(d * d, axis=-1, keepdims=True)
            yn = d * lax.rsqrt(var + GN_EPS) * gnw_ref[:, c:c + RET_HEAD_DIM]
            o_ref[rows, ATTN_WIDTH + c:ATTN_WIDTH + c + RET_HEAD_DIM] = (sg.astype(F32) * yn).astype(BF16)
            if hd % 2 == 1:
                yield


def _post_units(final, ar_ref, h_ref, wo_ref, n_ref, wg_ref, wu_ref, wd_ref, fn_ref, o_ref, hmid_ref):
    h2 = h_ref[...] + _dot(ar_ref[...], wo_ref[...])
    xn = _rms(h2, n_ref[...]).astype(BF16)
    yield
    for c in range(D_FF // FF_CHUNK):
        sl = slice(c * FF_CHUNK, (c + 1) * FF_CHUNK)
        g = _dot(xn, wg_ref[:, sl])
        u = _dot(xn, wu_ref[:, sl])
        hmid_ref[:, sl] = (jax.nn.silu(g) * u).astype(BF16)
        yield
    for c in range(D_MODEL // OUT_CHUNK):
        sl = slice(c * OUT_CHUNK, (c + 1) * OUT_CHUNK)
        o_ref[:, sl] = h2[:, sl] + 0.5 * _dot(hmid_ref[...], wd_ref[:, sl])
        yield
    if final:
        o_ref[...] = _rms(o_ref[...], fn_ref[...])


def _interleave(*gens):
    live = list(gens)
    while live:
        for g in list(live):
            try:
                next(g)
            except StopIteration:
                live.remove(g)


def _mixpost_kernel(final, sink_ref, decay_ref, act_ref, kvp_ref, dmat_ref, zeta_ref, xi_ref, gnw_ref,
                    h_ref, wo_ref, n_ref, wg_ref, wu_ref, wd_ref, fn_ref,
                    o_ref, state_ref, ar_cur_ref, ar_prev_ref, hmid_ref):
    t = pl.program_id(0)
    seq_start = (jnp.minimum(t, N_TILES - 1) % TILES_PER_SEQ) == 0

    @pl.when(t == 0)
    def _():
        ar_prev_ref[...] = jnp.zeros((TM, D_MODEL), BF16)

    @pl.when(seq_start)
    def _():
        state_ref[...] = jnp.zeros_like(state_ref)

    _interleave(
        _post_units(final, ar_prev_ref, h_ref, wo_ref, n_ref, wg_ref, wu_ref, wd_ref, fn_ref, o_ref, hmid_ref),
        _mix_units(seq_start, sink_ref, decay_ref, act_ref, kvp_ref, dmat_ref, zeta_ref, xi_ref, gnw_ref,
                   ar_cur_ref, state_ref))
    ar_prev_ref[...] = ar_cur_ref[...]


def _resident(shape, layer):
    nd = len(shape)
    return pl.BlockSpec((None,) + shape, lambda *_: (layer,) + (0,) * nd, pipeline_mode=pl.Buffered(1))


def _const(shape):
    nd = len(shape)
    return pl.BlockSpec(shape, lambda *_: (0,) * nd, pipeline_mode=pl.Buffered(1))


def _pre_call(layer, x, n1, wg, wu, wd, n2, win, cos_t, sin_t):
    tok = lambda w: pl.BlockSpec((TM, w), lambda i: (i, 0))
    return pl.pallas_call(
        _pre_kernel,
        grid=(N_TILES,),
        in_specs=[
            tok(D_MODEL),
            _resident((1, D_MODEL), layer),
            _resident((D_MODEL, D_FF), layer),
            _resident((D_MODEL, D_FF), layer),
            _resident((D_FF, D_MODEL), layer),
            _resident((1, D_MODEL), layer),
            _resident((D_MODEL, D_IN), layer),
            _const((SEQ, RET_HEAD_DIM)),
            _const((SEQ, RET_HEAD_DIM)),
        ],
        out_specs=[tok(D_MODEL), tok(D_ACT)],
        out_shape=[jax.ShapeDtypeStruct((N_TOK, D_MODEL), F32), jax.ShapeDtypeStruct((N_TOK, D_ACT), BF16)],
        scratch_shapes=[pltpu.VMEM((TM, D_FF), BF16)],
        compiler_params=pltpu.CompilerParams(dimension_semantics=("arbitrary",), vmem_limit_bytes=VMEM_LIMIT),
        name=f"pre{layer}",
    )(x, n1, wg, wu, wd, n2, win, cos_t, sin_t)


def _mixpost_call(layer, final, sinks, decay, act, dmat, zeta_b, xi_b, gnw, h, wo, n, wg, wu, wd, fn):
    def mix_tile(i):
        return jnp.minimum(i, N_TILES - 1)

    prev_kv = pl.BlockSpec((BLK, A_RQ - A_KD),
                           lambda i: (jnp.maximum(mix_tile(i) * BLKS_PER_TILE - 1, 0), A_KD // (A_RQ - A_KD)))
    lag = pl.BlockSpec((TM, D_MODEL), lambda i: (jnp.maximum(i - 1, 0), 0))
    smem = pl.BlockSpec(memory_space=pltpu.SMEM)
    return pl.pallas_call(
        functools.partial(_mixpost_kernel, final),
        grid=(N_TILES + 1,),
        in_specs=[smem, smem,
                  pl.BlockSpec((TM, D_ACT), lambda i: (mix_tile(i), 0)),
                  prev_kv,
                  _const((N_RET_HEADS, BLK, BLK)), _const((N_RET_HEADS, BLK, BLK)), _const((N_RET_HEADS, BLK, BLK)),
                  _resident((1, RET_WIDTH), layer),
                  lag,
                  _resident((D_MODEL, D_MODEL), layer),
                  _resident((1, D_MODEL), layer),
                  _resident((D_MODEL, D_FF), layer),
                  _resident((D_MODEL, D_FF), layer),
                  _resident((D_FF, D_MODEL), layer),
                  _const((1, D_MODEL))],
        out_specs=lag,
        out_shape=jax.ShapeDtypeStruct((N_TOK, D_MODEL), F32),
        scratch_shapes=[pltpu.VMEM((N_RET_HEADS, RET_HEAD_DIM, RET_HEAD_DIM), F32),
                        pltpu.VMEM((TM, D_MODEL), BF16),
                        pltpu.VMEM((TM, D_MODEL), BF16),
                        pltpu.VMEM((TM, D_FF), BF16)],
        compiler_params=pltpu.CompilerParams(dimension_semantics=("arbitrary",), vmem_limit_bytes=VMEM_LIMIT),
        name=f"mixpost{layer}",
    )(sinks, decay, act, act, dmat, zeta_b, xi_b, gnw, h, wo, n, wg, wu, wd, fn)


def _tables():
    pos = jnp.arange(SEQ, dtype=F32)
    inv_freq = ROPE_BASE ** (-jnp.arange(0, RET_HEAD_DIM, 2, dtype=F32) / RET_HEAD_DIM)
    ang = pos[:, None] * inv_freq[None, :]
    cos, sin = jnp.cos(ang), jnp.sin(ang)
    cos_t = jnp.concatenate([cos, cos], axis=-1)
    sin_t = jnp.concatenate([-sin, sin], axis=-1)
    log_gamma = jnp.log(1.0 - 2.0 ** (-5.0 - jnp.arange(N_RET_HEADS, dtype=F32)))
    idx = jnp.arange(BLK, dtype=F32)
    dif = idx[:, None] - idx[None, :]
    dmat = jnp.where(dif[None] >= 0, jnp.exp(jnp.maximum(dif, 0.0)[None] * log_gamma[:, None, None]), 0.0)
    zeta = jnp.exp((BLK - 1.0 - idx)[None, :] * log_gamma[:, None])
    xi = jnp.exp((idx + 1.0)[None, :] * log_gamma[:, None])
    decay = jnp.exp(BLK * log_gamma)
    bshape = (N_RET_HEADS, BLK, RET_HEAD_DIM)
    zeta_b = jnp.broadcast_to(zeta[:, :, None], bshape)
    xi_b = jnp.broadcast_to(xi[:, :, None], bshape)
    return cos_t, sin_t, dmat, zeta_b, xi_b, decay


def kernel(x, ffn1_norm, ffn1_w_gate, ffn1_w_up, ffn1_w_down, mix_norm, w_in, attn_sinks, ret_gn_w, w_out,
           ffn2_norm, ffn2_w_gate, ffn2_w_up, ffn2_w_down, final_norm):
    cos_t, sin_t, dmat, zeta_b, xi_b, decay = _tables()
    win = w_in.astype(BF16)
    wg1, wu1, wd1 = ffn1_w_gate.astype(BF16), ffn1_w_up.astype(BF16), ffn1_w_down.astype(BF16)
    wg2, wu2, wd2 = ffn2_w_gate.astype(BF16), ffn2_w_up.astype(BF16), ffn2_w_down.astype(BF16)
    wo = w_out.astype(BF16)
    n1 = ffn1_norm.reshape(DEPTH, 1, D_MODEL)
    nm = mix_norm.reshape(DEPTH, 1, D_MODEL)
    n2 = ffn2_norm.reshape(DEPTH, 1, D_MODEL)
    gnw = ret_gn_w.reshape(DEPTH, 1, RET_WIDTH)
    fn = final_norm.reshape(1, D_MODEL)

    h = x.reshape(N_TOK, D_MODEL)
    for layer in range(DEPTH):
        h, act = _pre_call(layer, h, n1, wg1, wu1, wd1, nm, win, cos_t, sin_t)
        h = _mixpost_call(layer, layer == DEPTH - 1, attn_sinks[layer], decay, act, dmat, zeta_b, xi_b, gnw,
                          h, wo, n2, wg2, wu2, wd2, fn)
    return h.reshape(BATCH, SEQ, D_MODEL)
```

```python
import functools

import jax
import jax.numpy as jnp
from jax import lax
from jax.experimental import pallas as pl
from jax.experimental.pallas import tpu as pltpu

D_MODEL = 1024
BATCH = 8
SEQ = 2048
DEPTH = 4
ATTN_WIDTH = 512
RET_WIDTH = 512
HEAD_DIM = 64
N_KV_HEADS = 2
GQA_GROUP = 4
KV_WIDTH = 128
WINDOW = 128
BLK = 128
N_RET_HEADS = 4
RET_HEAD_DIM = 128
ROPE_BASE = 10000.0
D_FF = 2816
D_IN = 2816
NORM_EPS = 1e-6
GN_EPS = 1e-5
NEG_INF = -1e30
N_TOK = BATCH * SEQ

W_AQ, W_KV, W_RQ, W_RK, W_RV, W_RG = 0, 512, 768, 1280, 1792, 2304
A_AQ, A_KD, A_VD, A_RQ, A_RK, A_RV, A_SG, D_ACT = 0, 512, 768, 1024, 1536, 2048, 2560, 3072

TM = 512
N_TILES = N_TOK // TM
TILES_PER_SEQ = SEQ // TM
BLKS_PER_TILE = TM // BLK
FF_SPLITS = ((0, 1536), (1536, 2816))
FF_CHUNK = 256
OUT_CHUNK = 256
CAST_ROWS = {D_MODEL: D_MODEL // N_TILES, D_FF: D_FF // (N_TILES // 2)}
VMEM_LIMIT = 56 * 1024 * 1024

F32 = jnp.float32
BF16 = jnp.bfloat16


def _dot(a, b):
    return jnp.dot(a, b, preferred_element_type=F32)


def _dot_nt(a, b):
    return lax.dot_general(a, b, (((1,), (1,)), ((), ())), preferred_element_type=F32)


def _dot_tn(a, b):
    return lax.dot_general(a, b, (((0,), (0,)), ((), ())), preferred_element_type=F32)


def _rms(x, w):
    y = x * lax.rsqrt(jnp.mean(x * x, axis=-1, keepdims=True) + NORM_EPS)
    return y * w


def _swap_halves(x):
    return pltpu.roll(x, x.shape[-1] // 2, 1)


def _cast_blocks(src_refs, dst_refs):
    for src, dst in zip(src_refs, dst_refs, strict=True):
        dst[...] = src[...].astype(BF16)


def _pre_kernel(x_ref, n1_ref, wg_ref, wu_ref, wd_ref, n2_ref, win_ref, cos_ref, sin_ref, *rest):
    cast_src, (h_ref, act_ref), cast_dst, hmid_ref = rest[:4], rest[4:6], rest[6:10], rest[10]
    _cast_blocks(cast_src, cast_dst)
    x = x_ref[...]
    xn = _rms(x, n1_ref[...]).astype(BF16)
    for lo, hi in FF_SPLITS:
        g = _dot(xn, wg_ref[:, lo:hi])
        u = _dot(xn, wu_ref[:, lo:hi])
        hmid_ref[:, lo:hi] = (jax.nn.silu(g) * u).astype(BF16)
    h = x + 0.5 * _dot(hmid_ref[...], wd_ref[...])
    h_ref[...] = h
    u = _rms(h, n2_ref[...]).astype(BF16)

    def proj(lo, hi):
        return _dot(u, win_ref[:, lo:hi])

    act_ref[:, A_AQ:A_KD] = (proj(W_AQ, W_KV) * (HEAD_DIM ** -0.5)).astype(BF16)

    kv = proj(W_KV, W_RQ)
    first = lax.broadcasted_iota(jnp.int32, (TM, KV_WIDTH), 1) < HEAD_DIM
    for src, dst in ((0, A_KD), (KV_WIDTH, A_VD)):
        t = kv[:, src:src + KV_WIDTH]
        ts = _swap_halves(t)
        act_ref[:, dst:dst + KV_WIDTH] = jnp.where(first, t, ts).astype(BF16)
        act_ref[:, dst + KV_WIDTH:dst + 2 * KV_WIDTH] = jnp.where(first, ts, t).astype(BF16)

    pos0 = pl.multiple_of((pl.program_id(0) % TILES_PER_SEQ) * TM, TM)
    cos = cos_ref[pl.ds(pos0, TM), :]
    sin = sin_ref[pl.ds(pos0, TM), :]
    q = proj(W_RQ, W_RK)
    k = proj(W_RK, W_RV)
    for hd in range(N_RET_HEADS):
        sl = slice(hd * RET_HEAD_DIM, (hd + 1) * RET_HEAD_DIM)
        qh = q[:, sl]
        kh = k[:, sl]
        act_ref[:, A_RQ + sl.start:A_RQ + sl.stop] = (qh * cos + _swap_halves(qh) * sin).astype(BF16)
        kr = kh * cos + _swap_halves(kh) * sin
        act_ref[:, A_RK + sl.start:A_RK + sl.stop] = (kr * (RET_HEAD_DIM ** -0.5)).astype(BF16)
    act_ref[:, A_RV:A_SG] = proj(W_RV, W_RG).astype(BF16)
    act_ref[:, A_SG:D_ACT] = jax.nn.silu(proj(W_RG, D_IN)).astype(BF16)


def _mix_units(seq_start, sink_ref, decay_ref, act_ref, kvp_ref, dmat_ref, zeta_ref, xi_ref, gnw_ref,
               o_ref, state_ref):
    row = lax.broadcasted_iota(jnp.int32, (BLK, 2 * BLK), 0)
    col = lax.broadcasted_iota(jnp.int32, (BLK, 2 * BLK), 1)
    band = (col > row) & (col <= row + WINDOW)
    band0 = band & ((col >= BLK) | jnp.logical_not(seq_start))
    lo = lax.broadcasted_iota(jnp.int32, (BLK, 2 * HEAD_DIM), 1) < HEAD_DIM
    zero_bf = jnp.zeros((BLK, 2 * HEAD_DIM), BF16)
    pair_w = 2 * HEAD_DIM

    for j in range(BLKS_PER_TILE):
        r0 = j * BLK
        rows = slice(r0, r0 + BLK)
        valid = band0 if j == 0 else band

        for kvh in range(N_KV_HEADS):
            kc = A_KD + kvh * pair_w
            vc = A_VD + kvh * pair_w
            if j == 0:
                k_h = jnp.concatenate([kvp_ref[:, kc - A_KD:kc - A_KD + pair_w], act_ref[0:BLK, kc:kc + pair_w]], axis=0)
                v_h = jnp.concatenate([kvp_ref[:, vc - A_KD:vc - A_KD + pair_w], act_ref[0:BLK, vc:vc + pair_w]], axis=0)
            else:
                k_h = act_ref[r0 - BLK:r0 + BLK, kc:kc + pair_w]
                v_h = act_ref[r0 - BLK:r0 + BLK, vc:vc + pair_w]
            qs = []
            for p in range(GQA_GROUP // 2):
                c0 = A_AQ + (kvh * 2 + p) * pair_w
                qp = act_ref[rows, c0:c0 + pair_w]
                qs.append(jnp.where(lo, qp, zero_bf))
                qs.append(jnp.where(lo, zero_bf, qp))
            s_all = _dot_nt(jnp.concatenate(qs, axis=0), k_h)
            ps, invs = [], []
            for g in range(GQA_GROUP):
                sink = sink_ref[kvh * GQA_GROUP + g]
                s = jnp.where(valid, s_all[g * BLK:(g + 1) * BLK, :], NEG_INF)
                m = jnp.maximum(jnp.max(s, axis=-1, keepdims=True), sink)
                e = jnp.exp(s - m)
                den = jnp.sum(e, axis=-1, keepdims=True) + jnp.exp(sink - m)
                ps.append(e.astype(BF16))
                invs.append(1.0 / den)
            o_all = _dot(jnp.concatenate(ps, axis=0), v_h)
            for p in range(GQA_GROUP // 2):
                c0 = (kvh * 2 + p) * pair_w
                oe = o_all[(2 * p) * BLK:(2 * p + 1) * BLK, :] * invs[2 * p]
                oo = o_all[(2 * p + 1) * BLK:(2 * p + 2) * BLK, :] * invs[2 * p + 1]
                o_ref[rows, c0:c0 + pair_w] = jnp.where(lo, oe, oo).astype(BF16)
            yield

        for hd in range(N_RET_HEADS):
            c = hd * RET_HEAD_DIM
            q = act_ref[rows, A_RQ + c:A_RQ + c + RET_HEAD_DIM]
            k = act_ref[rows, A_RK + c:A_RK + c + RET_HEAD_DIM]
            v = act_ref[rows, A_RV + c:A_RV + c + RET_HEAD_DIM]
            sg = act_ref[rows, A_SG + c:A_SG + c + RET_HEAD_DIM]
            st = state_ref[hd]
            sc = _dot_nt(q, k) * dmat_ref[hd]
            y = _dot(sc.astype(BF16), v) + _dot(q, st.astype(BF16)) * xi_ref[hd]
            vz = (v.astype(F32) * zeta_ref[hd]).astype(BF16)
            state_ref[hd] = st * decay_ref[hd] + _dot_tn(k, vz)
            mu = jnp.mean(y, axis=-1, keepdims=True)
            d = y - mu
            var = jnp.mean(d * d, axis=-1, keepdims=True)
            yn = d * lax.rsqrt(var + GN_EPS) * gnw_ref[:, c:c + RET_HEAD_DIM]
            o_ref[rows, ATTN_WIDTH + c:ATTN_WIDTH + c + RET_HEAD_DIM] = (sg.astype(F32) * yn).astype(BF16)
            if hd % 2 == 1:
                yield


def _post_units(final, ar_ref, h_ref, wo_ref, n_ref, wg_ref, wu_ref, wd_ref, fn_ref, o_ref, hmid_ref):
    h2 = h_ref[...] + _dot(ar_ref[...], wo_ref[...])
    xn = _rms(h2, n_ref[...]).astype(BF16)
    yield
    for c in range(D_FF // FF_CHUNK):
        sl = slice(c * FF_CHUNK, (c + 1) * FF_CHUNK)
        g = _dot(xn, wg_ref[:, sl])
        u = _dot(xn, wu_ref[:, sl])
        hmid_ref[:, sl] = (jax.nn.silu(g) * u).astype(BF16)
        yield
    for c in range(D_MODEL // OUT_CHUNK):
        sl = slice(c * OUT_CHUNK, (c + 1) * OUT_CHUNK)
        o_ref[:, sl] = h2[:, sl] + 0.5 * _dot(hmid_ref[...], wd_ref[:, sl])
        yield
    if final:
        o_ref[...] = _rms(o_ref[...], fn_ref[...])


def _interleave(*gens):
    live = list(gens)
    while live:
        for g in list(live):
            try:
                next(g)
            except StopIteration:
                live.remove(g)


def _mixpost_kernel(final, sink_ref, decay_ref, act_ref, kvp_ref, dmat_ref, zeta_ref, xi_ref, gnw_ref,
                    h_ref, wo_ref, n_ref, wg_ref, wu_ref, wd_ref, fn_ref, *rest):
    n_cast = 0 if final else 4
    cast_src, o_ref, cast_dst = rest[:n_cast], rest[n_cast], rest[n_cast + 1:2 * n_cast + 1]
    state_ref, ar_cur_ref, ar_prev_ref, hmid_ref = rest[2 * n_cast + 1:]
    t = pl.program_id(0)
    seq_start = (jnp.minimum(t, N_TILES - 1) % TILES_PER_SEQ) == 0

    @pl.when(t == 0)
    def _():
        ar_prev_ref[...] = jnp.zeros((TM, D_MODEL), BF16)

    @pl.when(seq_start)
    def _():
        state_ref[...] = jnp.zeros_like(state_ref)

    _cast_blocks(cast_src, cast_dst)
    _interleave(
        _post_units(final, ar_prev_ref, h_ref, wo_ref, n_ref, wg_ref, wu_ref, wd_ref, fn_ref, o_ref, hmid_ref),
        _mix_units(seq_start, sink_ref, decay_ref, act_ref, kvp_ref, dmat_ref, zeta_ref, xi_ref, gnw_ref,
                   ar_cur_ref, state_ref))
    ar_prev_ref[...] = ar_cur_ref[...]


def _resident(shape, layer):
    nd = len(shape)
    return pl.BlockSpec((None,) + shape, lambda *_: (layer,) + (0,) * nd, pipeline_mode=pl.Buffered(1))


def _const(shape):
    nd = len(shape)
    return pl.BlockSpec(shape, lambda *_: (0,) * nd, pipeline_mode=pl.Buffered(1))


def _cast_specs(layer, shapes):
    ins, outs, shapes_out = [], [], []
    for rows, cols in shapes:
        blk = CAST_ROWS[rows]
        last = rows // blk - 1
        ins.append(pl.BlockSpec((None, blk, cols), lambda i, last=last: (layer, jnp.minimum(i, last), 0)))
        outs.append(pl.BlockSpec((blk, cols), lambda i, last=last: (jnp.minimum(i, last), 0)))
        shapes_out.append(jax.ShapeDtypeStruct((rows, cols), BF16))
    return ins, outs, shapes_out


def _pre_call(layer, x, n1, wg, wu, wd, n2, win, cos_t, sin_t, next_f32):
    tok = lambda w: pl.BlockSpec((TM, w), lambda i: (i, 0))
    cast_in, cast_out, cast_shapes = _cast_specs(
        layer, ((D_MODEL, D_MODEL), (D_MODEL, D_FF), (D_MODEL, D_FF), (D_FF, D_MODEL)))
    return pl.pallas_call(
        _pre_kernel,
        grid=(N_TILES,),
        in_specs=[
            tok(D_MODEL),
            _resident((1, D_MODEL), layer),
            _const((D_MODEL, D_FF)),
            _const((D_MODEL, D_FF)),
            _const((D_FF, D_MODEL)),
            _resident((1, D_MODEL), layer),
            _const((D_MODEL, D_IN)),
            _const((SEQ, RET_HEAD_DIM)),
            _const((SEQ, RET_HEAD_DIM)),
        ] + cast_in,
        out_specs=[tok(D_MODEL), tok(D_ACT)] + cast_out,
        out_shape=[jax.ShapeDtypeStruct((N_TOK, D_MODEL), F32),
                   jax.ShapeDtypeStruct((N_TOK, D_ACT), BF16)] + cast_shapes,
        scratch_shapes=[pltpu.VMEM((TM, D_FF), BF16)],
        compiler_params=pltpu.CompilerParams(dimension_semantics=("arbitrary",), vmem_limit_bytes=VMEM_LIMIT),
        name=f"pre{layer}",
    )(x, n1, wg, wu, wd, n2, win, cos_t, sin_t, *next_f32)


def _mixpost_call(layer, final, sinks, decay, act, dmat, zeta_b, xi_b, gnw, h, wo, n, wg, wu, wd, fn, next_f32):
    def mix_tile(i):
        return jnp.minimum(i, N_TILES - 1)

    if final:
        cast_in, cast_out, cast_shapes = [], [], []
    else:
        cast_in, cast_out, cast_shapes = _cast_specs(
            layer + 1, ((D_MODEL, D_FF), (D_MODEL, D_FF), (D_FF, D_MODEL), (D_MODEL, D_IN)))

    prev_kv = pl.BlockSpec((BLK, A_RQ - A_KD),
                           lambda i: (jnp.maximum(mix_tile(i) * BLKS_PER_TILE - 1, 0), A_KD // (A_RQ - A_KD)))
    lag = pl.BlockSpec((TM, D_MODEL), lambda i: (jnp.maximum(i - 1, 0), 0))
    smem = pl.BlockSpec(memory_space=pltpu.SMEM)
    return pl.pallas_call(
        functools.partial(_mixpost_kernel, final),
        grid=(N_TILES + 1,),
        in_specs=[smem, smem,
                  pl.BlockSpec((TM, D_ACT), lambda i: (mix_tile(i), 0)),
                  prev_kv,
                  _const((N_RET_HEADS, BLK, BLK)), _const((N_RET_HEADS, BLK, BLK)), _const((N_RET_HEADS, BLK, BLK)),
                  _resident((1, RET_WIDTH), layer),
                  lag,
                  _const((D_MODEL, D_MODEL)),
                  _resident((1, D_MODEL), layer),
                  _const((D_MODEL, D_FF)),
                  _const((D_MODEL, D_FF)),
                  _const((D_FF, D_MODEL)),
                  _const((1, D_MODEL))] + cast_in,
        out_specs=[lag] + cast_out,
        out_shape=[jax.ShapeDtypeStruct((N_TOK, D_MODEL), F32)] + cast_shapes,
        scratch_shapes=[pltpu.VMEM((N_RET_HEADS, RET_HEAD_DIM, RET_HEAD_DIM), F32),
                        pltpu.VMEM((TM, D_MODEL), BF16),
                        pltpu.VMEM((TM, D_MODEL), BF16),
                        pltpu.VMEM((TM, D_FF), BF16)],
        compiler_params=pltpu.CompilerParams(dimension_semantics=("arbitrary",), vmem_limit_bytes=VMEM_LIMIT),
        name=f"mixpost{layer}",
    )(sinks, decay, act, act, dmat, zeta_b, xi_b, gnw, h, wo, n, wg, wu, wd, fn, *next_f32)


def _tables():
    pos = jnp.arange(SEQ, dtype=F32)
    inv_freq = ROPE_BASE ** (-jnp.arange(0, RET_HEAD_DIM, 2, dtype=F32) / RET_HEAD_DIM)
    ang = pos[:, None] * inv_freq[None, :]
    cos, sin = jnp.cos(ang), jnp.sin(ang)
    cos_t = jnp.concatenate([cos, cos], axis=-1)
    sin_t = jnp.concatenate([-sin, sin], axis=-1)
    log_gamma = jnp.log(1.0 - 2.0 ** (-5.0 - jnp.arange(N_RET_HEADS, dtype=F32)))
    idx = jnp.arange(BLK, dtype=F32)
    dif = idx[:, None] - idx[None, :]
    dmat = jnp.where(dif[None] >= 0, jnp.exp(jnp.maximum(dif, 0.0)[None] * log_gamma[:, None, None]), 0.0)
    zeta = jnp.exp((BLK - 1.0 - idx)[None, :] * log_gamma[:, None])
    xi = jnp.exp((idx + 1.0)[None, :] * log_gamma[:, None])
    decay = jnp.exp(BLK * log_gamma)
    bshape = (N_RET_HEADS, BLK, RET_HEAD_DIM)
    zeta_b = jnp.broadcast_to(zeta[:, :, None], bshape)
    xi_b = jnp.broadcast_to(xi[:, :, None], bshape)
    return cos_t, sin_t, dmat, zeta_b, xi_b, decay


def kernel(x, ffn1_norm, ffn1_w_gate, ffn1_w_up, ffn1_w_down, mix_norm, w_in, attn_sinks, ret_gn_w, w_out,
           ffn2_norm, ffn2_w_gate, ffn2_w_up, ffn2_w_down, final_norm):
    cos_t, sin_t, dmat, zeta_b, xi_b, decay = _tables()
    pre_w = [w[0].astype(BF16) for w in (ffn1_w_gate, ffn1_w_up, ffn1_w_down, w_in)]
    pre_f32 = (ffn1_w_gate, ffn1_w_up, ffn1_w_down, w_in)
    post_f32 = (w_out, ffn2_w_gate, ffn2_w_up, ffn2_w_down)
    n1 = ffn1_norm.reshape(DEPTH, 1, D_MODEL)
    nm = mix_norm.reshape(DEPTH, 1, D_MODEL)
    n2 = ffn2_norm.reshape(DEPTH, 1, D_MODEL)
    gnw = ret_gn_w.reshape(DEPTH, 1, RET_WIDTH)
    fn = final_norm.reshape(1, D_MODEL)

    h = x.reshape(N_TOK, D_MODEL)
    for layer in range(DEPTH):
        final = layer == DEPTH - 1
        wg1, wu1, wd1, win = pre_w
        h, act, wo, wg2, wu2, wd2 = _pre_call(layer, h, n1, wg1, wu1, wd1, nm, win, cos_t, sin_t, post_f32)
        h, *pre_w = _mixpost_call(layer, final, attn_sinks[layer], decay, act, dmat, zeta_b, xi_b, gnw,
                                  h, wo, n2, wg2, wu2, wd2, fn, () if final else pre_f32)
    return h.reshape(BATCH, SEQ, D_MODEL)
```

```python
import functools

import jax
import jax.numpy as jnp
from jax import lax
from jax.experimental import pallas as pl
from jax.experimental.pallas import tpu as pltpu

D_MODEL = 1024
BATCH = 8
SEQ = 2048
DEPTH = 4
ATTN_WIDTH = 512
RET_WIDTH = 512
HEAD_DIM = 64
N_KV_HEADS = 2
GQA_GROUP = 4
KV_WIDTH = 128
WINDOW = 128
BLK = 128
N_RET_HEADS = 4
RET_HEAD_DIM = 128
ROPE_BASE = 10000.0
D_FF = 2816
D_IN = 2816
NORM_EPS = 1e-6
GN_EPS = 1e-5
NEG_INF = -1e30
N_TOK = BATCH * SEQ

W_AQ, W_KV, W_RQ, W_RK, W_RV, W_RG = 0, 512, 768, 1280, 1792, 2304
A_AQ, A_KD, A_VD, A_RQ, A_RK, A_RV, A_SG, D_ACT = 0, 512, 768, 1024, 1536, 2048, 2560, 3072

TM = 512
N_TILES = N_TOK // TM
TILES_PER_SEQ = SEQ // TM
BLKS_PER_TILE = TM // BLK
FF_SPLITS = ((0, 1536), (1536, 2816))
FF_CHUNK = 256
OUT_CHUNK = 256
MIX_STAGES = 3
N_MIX_SLOTS = 4 * BLKS_PER_TILE + MIX_STAGES - 1
N_POST_UNITS = 1 + 2 * (D_FF // FF_CHUNK) + D_MODEL // OUT_CHUNK
CAST_ROWS = {D_MODEL: D_MODEL // N_TILES, D_FF: D_FF // (N_TILES // 2)}
VMEM_LIMIT = 56 * 1024 * 1024

F32 = jnp.float32
BF16 = jnp.bfloat16


def _dot(a, b):
    return jnp.dot(a, b, preferred_element_type=F32)


def _dot_nt(a, b):
    return lax.dot_general(a, b, (((1,), (1,)), ((), ())), preferred_element_type=F32)


def _dot_tn(a, b):
    return lax.dot_general(a, b, (((0,), (0,)), ((), ())), preferred_element_type=F32)


def _rms(x, w):
    y = x * lax.rsqrt(jnp.mean(x * x, axis=-1, keepdims=True) + NORM_EPS)
    return y * w


def _swap_halves(x):
    return pltpu.roll(x, x.shape[-1] // 2, 1)


def _cast_blocks(src_refs, dst_refs):
    for src, dst in zip(src_refs, dst_refs, strict=True):
        dst[...] = src[...].astype(BF16)


def _pre_kernel(x_ref, n1_ref, wg_ref, wu_ref, wd_ref, n2_ref, win_ref, cos_ref, sin_ref, *rest):
    cast_src, (h_ref, act_ref), cast_dst, hmid_ref = rest[:4], rest[4:6], rest[6:10], rest[10]
    _cast_blocks(cast_src, cast_dst)
    x = x_ref[...]
    xn = _rms(x, n1_ref[...]).astype(BF16)
    for lo, hi in FF_SPLITS:
        g = _dot(xn, wg_ref[:, lo:hi])
        u = _dot(xn, wu_ref[:, lo:hi])
        hmid_ref[:, lo:hi] = (jax.nn.silu(g) * u).astype(BF16)
    h = x + 0.5 * _dot(hmid_ref[...], wd_ref[...])
    h_ref[...] = h
    u = _rms(h, n2_ref[...]).astype(BF16)

    def proj(lo, hi):
        return _dot(u, win_ref[:, lo:hi])

    act_ref[:, A_AQ:A_KD] = (proj(W_AQ, W_KV) * (HEAD_DIM ** -0.5)).astype(BF16)

    kv = proj(W_KV, W_RQ)
    first = lax.broadcasted_iota(jnp.int32, (TM, KV_WIDTH), 1) < HEAD_DIM
    for src, dst in ((0, A_KD), (KV_WIDTH, A_VD)):
        t = kv[:, src:src + KV_WIDTH]
        ts = _swap_halves(t)
        act_ref[:, dst:dst + KV_WIDTH] = jnp.where(first, t, ts).astype(BF16)
        act_ref[:, dst + KV_WIDTH:dst + 2 * KV_WIDTH] = jnp.where(first, ts, t).astype(BF16)

    pos0 = pl.multiple_of((pl.program_id(0) % TILES_PER_SEQ) * TM, TM)
    cos = cos_ref[pl.ds(pos0, TM), :]
    sin = sin_ref[pl.ds(pos0, TM), :]
    q = proj(W_RQ, W_RK)
    k = proj(W_RK, W_RV)
    for hd in range(N_RET_HEADS):
        sl = slice(hd * RET_HEAD_DIM, (hd + 1) * RET_HEAD_DIM)
        qh = q[:, sl]
        kh = k[:, sl]
        act_ref[:, A_RQ + sl.start:A_RQ + sl.stop] = (qh * cos + _swap_halves(qh) * sin).astype(BF16)
        kr = kh * cos + _swap_halves(kh) * sin
        act_ref[:, A_RK + sl.start:A_RK + sl.stop] = (kr * (RET_HEAD_DIM ** -0.5)).astype(BF16)
    act_ref[:, A_RV:A_SG] = proj(W_RV, W_RG).astype(BF16)
    act_ref[:, A_SG:D_ACT] = jax.nn.silu(proj(W_RG, D_IN)).astype(BF16)


def _mix_units(seq_start, sink_ref, decay_ref, act_ref, kvp_ref, dmat_ref, zeta_ref, xi_ref, gnw_ref,
               o_ref, state_ref):
    row = lax.broadcasted_iota(jnp.int32, (BLK, 2 * BLK), 0)
    col = lax.broadcasted_iota(jnp.int32, (BLK, 2 * BLK), 1)
    band = (col > row) & (col <= row + WINDOW)
    band0 = band & ((col >= BLK) | jnp.logical_not(seq_start))
    lo = lax.broadcasted_iota(jnp.int32, (BLK, 2 * HEAD_DIM), 1) < HEAD_DIM
    zero_bf = jnp.zeros((BLK, 2 * HEAD_DIM), BF16)
    pair_w = 2 * HEAD_DIM

    def attention(j, kvh):
        r0 = j * BLK
        rows = slice(r0, r0 + BLK)
        valid = band0 if j == 0 else band
        kc = A_KD + kvh * pair_w
        vc = A_VD + kvh * pair_w
        if j == 0:
            k_h = jnp.concatenate([kvp_ref[:, kc - A_KD:kc - A_KD + pair_w], act_ref[0:BLK, kc:kc + pair_w]], axis=0)
            v_h = jnp.concatenate([kvp_ref[:, vc - A_KD:vc - A_KD + pair_w], act_ref[0:BLK, vc:vc + pair_w]], axis=0)
        else:
            k_h = act_ref[r0 - BLK:r0 + BLK, kc:kc + pair_w]
            v_h = act_ref[r0 - BLK:r0 + BLK, vc:vc + pair_w]
        qs = []
        for p in range(GQA_GROUP // 2):
            c0 = A_AQ + (kvh * 2 + p) * pair_w
            qp = act_ref[rows, c0:c0 + pair_w]
            qs.append(jnp.where(lo, qp, zero_bf))
            qs.append(jnp.where(lo, zero_bf, qp))
        s_all = _dot_nt(jnp.concatenate(qs, axis=0), k_h)
        yield
        ps, invs = [], []
        for g in range(GQA_GROUP):
            sink = sink_ref[kvh * GQA_GROUP + g]
            s = jnp.where(valid, s_all[g * BLK:(g + 1) * BLK, :], NEG_INF)
            m = jnp.maximum(jnp.max(s, axis=-1, keepdims=True), sink)
            e = jnp.exp(s - m)
            den = jnp.sum(e, axis=-1, keepdims=True) + jnp.exp(sink - m)
            ps.append(e.astype(BF16))
            invs.append(1.0 / den)
        o_all = _dot(jnp.concatenate(ps, axis=0), v_h)
        yield
        for p in range(GQA_GROUP // 2):
            c0 = (kvh * 2 + p) * pair_w
            oe = o_all[(2 * p) * BLK:(2 * p + 1) * BLK, :] * invs[2 * p]
            oo = o_all[(2 * p + 1) * BLK:(2 * p + 2) * BLK, :] * invs[2 * p + 1]
            o_ref[rows, c0:c0 + pair_w] = jnp.where(lo, oe, oo).astype(BF16)

    def retention(j, heads):
        rows = slice(j * BLK, (j + 1) * BLK)
        part = []
        for hd in heads:
            c = hd * RET_HEAD_DIM
            q = act_ref[rows, A_RQ + c:A_RQ + c + RET_HEAD_DIM]
            k = act_ref[rows, A_RK + c:A_RK + c + RET_HEAD_DIM]
            v = act_ref[rows, A_RV + c:A_RV + c + RET_HEAD_DIM]
            vz = (v.astype(F32) * zeta_ref[hd]).astype(BF16)
            part.append((v, _dot_nt(q, k), _dot(q, state_ref[hd].astype(BF16)), _dot_tn(k, vz)))
        yield
        intra = []
        for hd, (v, sc, _, kv) in zip(heads, part, strict=True):
            state_ref[hd] = state_ref[hd] * decay_ref[hd] + kv
            intra.append(_dot((sc * dmat_ref[hd]).astype(BF16), v))
        yield
        for hd, (_, _, cross, _), y_intra in zip(heads, part, intra, strict=True):
            c = hd * RET_HEAD_DIM
            y = y_intra + cross * xi_ref[hd]
            mu = jnp.mean(y, axis=-1, keepdims=True)
            d = y - mu
            var = jnp.mean(d * d, axis=-1, keepdims=True)
            yn = d * lax.rsqrt(var + GN_EPS) * gnw_ref[:, c:c + RET_HEAD_DIM]
            sg = act_ref[rows, A_SG + c:A_SG + c + RET_HEAD_DIM]
            o_ref[rows, ATTN_WIDTH + c:ATTN_WIDTH + c + RET_HEAD_DIM] = (sg.astype(F32) * yn).astype(BF16)

    chains = []
    for j in range(BLKS_PER_TILE):
        chains += [attention(j, 0), attention(j, 1), retention(j, (0, 1)), retention(j, (2, 3))]
    active = []
    for chain in chains + [None] * (MIX_STAGES - 1):
        if chain is not None:
            active.append(chain)
        for g in list(active):
            try:
                next(g)
            except StopIteration:
                active.remove(g)
        yield


def _post_units(final, ar_ref, h_ref, wo_ref, n_ref, wg_ref, wu_ref, wd_ref, fn_ref, o_ref, hmid_ref):
    h2 = h_ref[...] + _dot(ar_ref[...], wo_ref[...])
    xn = _rms(h2, n_ref[...]).astype(BF16)
    yield
    for c in range(D_FF // FF_CHUNK):
        sl = slice(c * FF_CHUNK, (c + 1) * FF_CHUNK)
        g = _dot(xn, wg_ref[:, sl])
        yield
        u = _dot(xn, wu_ref[:, sl])
        hmid_ref[:, sl] = (jax.nn.silu(g) * u).astype(BF16)
        yield
    for c in range(D_MODEL // OUT_CHUNK):
        sl = slice(c * OUT_CHUNK, (c + 1) * OUT_CHUNK)
        o_ref[:, sl] = h2[:, sl] + 0.5 * _dot(hmid_ref[...], wd_ref[:, sl])
        yield
    if final:
        o_ref[...] = _rms(o_ref[...], fn_ref[...])


def _interleave(*streams):
    total = max(n for _, n in streams)
    done = [0] * len(streams)
    for k in range(1, total + 1):
        for idx, (gen, n) in enumerate(streams):
            while done[idx] < -(-k * n // total):
                next(gen)
                done[idx] += 1
    for gen, _ in streams:
        for _ in gen:
            raise AssertionError("stream yielded more often than declared")


def _mixpost_kernel(final, sink_ref, decay_ref, act_ref, kvp_ref, dmat_ref, zeta_ref, xi_ref, gnw_ref,
                    h_ref, wo_ref, n_ref, wg_ref, wu_ref, wd_ref, fn_ref, *rest):
    n_cast = 0 if final else 4
    cast_src, o_ref, cast_dst = rest[:n_cast], rest[n_cast], rest[n_cast + 1:2 * n_cast + 1]
    state_ref, ar_cur_ref, ar_prev_ref, hmid_ref = rest[2 * n_cast + 1:]
    t = pl.program_id(0)
    seq_start = (jnp.minimum(t, N_TILES - 1) % TILES_PER_SEQ) == 0

    @pl.when(t == 0)
    def _():
        ar_prev_ref[...] = jnp.zeros((TM, D_MODEL), BF16)

    @pl.when(seq_start)
    def _():
        state_ref[...] = jnp.zeros_like(state_ref)

    _cast_blocks(cast_src, cast_dst)
    _interleave(
        (_post_units(final, ar_prev_ref, h_ref, wo_ref, n_ref, wg_ref, wu_ref, wd_ref, fn_ref, o_ref, hmid_ref),
         N_POST_UNITS),
        (_mix_units(seq_start, sink_ref, decay_ref, act_ref, kvp_ref, dmat_ref, zeta_ref, xi_ref, gnw_ref,
                    ar_cur_ref, state_ref), N_MIX_SLOTS))
    ar_prev_ref[...] = ar_cur_ref[...]


def _resident(shape, layer):
    nd = len(shape)
    return pl.BlockSpec((None,) + shape, lambda *_: (layer,) + (0,) * nd, pipeline_mode=pl.Buffered(1))


def _const(shape):
    nd = len(shape)
    return pl.BlockSpec(shape, lambda *_: (0,) * nd, pipeline_mode=pl.Buffered(1))


def _cast_specs(layer, shapes):
    ins, outs, shapes_out = [], [], []
    for rows, cols in shapes:
        blk = CAST_ROWS[rows]
        last = rows // blk - 1
        ins.append(pl.BlockSpec((None, blk, cols), lambda i, last=last: (layer, jnp.minimum(i, last), 0)))
        outs.append(pl.BlockSpec((blk, cols), lambda i, last=last: (jnp.minimum(i, last), 0)))
        shapes_out.append(jax.ShapeDtypeStruct((rows, cols), BF16))
    return ins, outs, shapes_out


def _pre_call(layer, x, n1, wg, wu, wd, n2, win, cos_t, sin_t, next_f32):
    tok = lambda w: pl.BlockSpec((TM, w), lambda i: (i, 0))
    cast_in, cast_out, cast_shapes = _cast_specs(
        layer, ((D_MODEL, D_MODEL), (D_MODEL, D_FF), (D_MODEL, D_FF), (D_FF, D_MODEL)))
    return pl.pallas_call(
        _pre_kernel,
        grid=(N_TILES,),
        in_specs=[
            tok(D_MODEL),
            _resident((1, D_MODEL), layer),
            _const((D_MODEL, D_FF)),
            _const((D_MODEL, D_FF)),
            _const((D_FF, D_MODEL)),
            _resident((1, D_MODEL), layer),
            _const((D_MODEL, D_IN)),
            _const((SEQ, RET_HEAD_DIM)),
            _const((SEQ, RET_HEAD_DIM)),
        ] + cast_in,
        out_specs=[tok(D_MODEL), tok(D_ACT)] + cast_out,
        out_shape=[jax.ShapeDtypeStruct((N_TOK, D_MODEL), F32),
                   jax.ShapeDtypeStruct((N_TOK, D_ACT), BF16)] + cast_shapes,
        scratch_shapes=[pltpu.VMEM((TM, D_FF), BF16)],
        compiler_params=pltpu.CompilerParams(dimension_semantics=("arbitrary",), vmem_limit_bytes=VMEM_LIMIT),
        name=f"pre{layer}",
    )(x, n1, wg, wu, wd, n2, win, cos_t, sin_t, *next_f32)


def _mixpost_call(layer, final, sinks, decay, act, dmat, zeta_b, xi_b, gnw, h, wo, n, wg, wu, wd, fn, next_f32):
    def mix_tile(i):
        return jnp.minimum(i, N_TILES - 1)

    if final:
        cast_in, cast_out, cast_shapes = [], [], []
    else:
        cast_in, cast_out, cast_shapes = _cast_specs(
            layer + 1, ((D_MODEL, D_FF), (D_MODEL, D_FF), (D_FF, D_MODEL), (D_MODEL, D_IN)))

    prev_kv = pl.BlockSpec((BLK, A_RQ - A_KD),
                           lambda i: (jnp.maximum(mix_tile(i) * BLKS_PER_TILE - 1, 0), A_KD // (A_RQ - A_KD)))
    lag = pl.BlockSpec((TM, D_MODEL), lambda i: (jnp.maximum(i - 1, 0), 0))
    smem = pl.BlockSpec(memory_space=pltpu.SMEM)
    return pl.pallas_call(
        functools.partial(_mixpost_kernel, final),
        grid=(N_TILES + 1,),
        in_specs=[smem, smem,
                  pl.BlockSpec((TM, D_ACT), lambda i: (mix_tile(i), 0)),
                  prev_kv,
                  _const((N_RET_HEADS, BLK, BLK)), _const((N_RET_HEADS, BLK, BLK)), _const((N_RET_HEADS, BLK, BLK)),
                  _resident((1, RET_WIDTH), layer),
                  lag,
                  _const((D_MODEL, D_MODEL)),
                  _resident((1, D_MODEL), layer),
                  _const((D_MODEL, D_FF)),
                  _const((D_MODEL, D_FF)),
                  _const((D_FF, D_MODEL)),
                  _const((1, D_MODEL))] + cast_in,
        out_specs=[lag] + cast_out,
        out_shape=[jax.ShapeDtypeStruct((N_TOK, D_MODEL), F32)] + cast_shapes,
        scratch_shapes=[pltpu.VMEM((N_RET_HEADS, RET_HEAD_DIM, RET_HEAD_DIM), F32),
                        pltpu.VMEM((TM, D_MODEL), BF16),
                        pltpu.VMEM((TM, D_MODEL), BF16),
                        pltpu.VMEM((TM, D_FF), BF16)],
        compiler_params=pltpu.CompilerParams(dimension_semantics=("arbitrary",), vmem_limit_bytes=VMEM_LIMIT),
        name=f"mixpost{layer}",
    )(sinks, decay, act, act, dmat, zeta_b, xi_b, gnw, h, wo, n, wg, wu, wd, fn, *next_f32)


def _tables():
    pos = jnp.arange(SEQ, dtype=F32)
    inv_freq = ROPE_BASE ** (-jnp.arange(0, RET_HEAD_DIM, 2, dtype=F32) / RET_HEAD_DIM)
    ang = pos[:, None] * inv_freq[None, :]
    cos, sin = jnp.cos(ang), jnp.sin(ang)
    cos_t = jnp.concatenate([cos, cos], axis=-1)
    sin_t = jnp.concatenate([-sin, sin], axis=-1)
    log_gamma = jnp.log(1.0 - 2.0 ** (-5.0 - jnp.arange(N_RET_HEADS, dtype=F32)))
    idx = jnp.arange(BLK, dtype=F32)
    dif = idx[:, None] - idx[None, :]
    dmat = jnp.where(dif[None] >= 0, jnp.exp(jnp.maximum(dif, 0.0)[None] * log_gamma[:, None, None]), 0.0)
    zeta = jnp.exp((BLK - 1.0 - idx)[None, :] * log_gamma[:, None])
    xi = jnp.exp((idx + 1.0)[None, :] * log_gamma[:, None])
    decay = jnp.exp(BLK * log_gamma)
    bshape = (N_RET_HEADS, BLK, RET_HEAD_DIM)
    zeta_b = jnp.broadcast_to(zeta[:, :, None], bshape)
    xi_b = jnp.broadcast_to(xi[:, :, None], bshape)
    return cos_t, sin_t, dmat, zeta_b, xi_b, decay


def kernel(x, ffn1_norm, ffn1_w_gate, ffn1_w_up, ffn1_w_down, mix_norm, w_in, attn_sinks, ret_gn_w, w_out,
           ffn2_norm, ffn2_w_gate, ffn2_w_up, ffn2_w_down, final_norm):
    cos_t, sin_t, dmat, zeta_b, xi_b, decay = _tables()
    pre_w = [w[0].astype(BF16) for w in (ffn1_w_gate, ffn1_w_up, ffn1_w_down, w_in)]
    pre_f32 = (ffn1_w_gate, ffn1_w_up, ffn1_w_down, w_in)
    post_f32 = (w_out, ffn2_w_gate, ffn2_w_up, ffn2_w_down)
    n1 = ffn1_norm.reshape(DEPTH, 1, D_MODEL)
    nm = mix_norm.reshape(DEPTH, 1, D_MODEL)
    n2 = ffn2_norm.reshape(DEPTH, 1, D_MODEL)
    gnw = ret_gn_w.reshape(DEPTH, 1, RET_WIDTH)
    fn = final_norm.reshape(1, D_MODEL)

    h = x.reshape(N_TOK, D_MODEL)
    for layer in range(DEPTH):
        final = layer == DEPTH - 1
        wg1, wu1, wd1, win = pre_w
        h, act, wo, wg2, wu2, wd2 = _pre_call(layer, h, n1, wg1, wu1, wd1, nm, win, cos_t, sin_t, post_f32)
        h, *pre_w = _mixpost_call(layer, final, attn_sinks[layer], decay, act, dmat, zeta_b, xi_b, gnw,
                                  h, wo, n2, wg2, wu2, wd2, fn, () if final else pre_f32)
    return h.reshape(BATCH, SEQ, D_MODEL)
```

```python
import functools

import jax
import jax.numpy as jnp
from jax import lax
from jax.experimental import pallas as pl
from jax.experimental.pallas import tpu as pltpu

D_MODEL = 1024
BATCH = 8
SEQ = 2048
DEPTH = 4
ATTN_WIDTH = 512
RET_WIDTH = 512
HEAD_DIM = 64
N_KV_HEADS = 2
GQA_GROUP = 4
KV_WIDTH = 128
WINDOW = 128
BLK = 128
N_RET_HEADS = 4
RET_HEAD_DIM = 128
ROPE_BASE = 10000.0
D_FF = 2816
D_IN = 2816
NORM_EPS = 1e-6
GN_EPS = 1e-5
NEG_INF = -1e30
N_TOK = BATCH * SEQ

W_AQ, W_KV, W_RQ, W_RK, W_RV, W_RG = 0, 512, 768, 1280, 1792, 2304
A_AQ, A_KD, A_VD, A_RQ, A_RK, A_RV, A_SG, D_ACT = 0, 512, 768, 1024, 1536, 2048, 2560, 3072

TM = 512
N_TILES = N_TOK // TM
TILES_PER_SEQ = SEQ // TM
BLKS_PER_TILE = TM // BLK
FF_CHUNK = 256
OUT_CHUNK = 512
N_PRE_UNITS = 1 + D_FF // FF_CHUNK + D_MODEL // OUT_CHUNK + 6
PRE_SKEW = 3
MIX_STAGES = 3
N_MIX_SLOTS = 4 * BLKS_PER_TILE + MIX_STAGES - 1
N_POST_UNITS = 1 + D_FF // FF_CHUNK + D_MODEL // OUT_CHUNK
POST_SKEW = 2
MIX_DELAY = 1
EARLY_ROUNDS = 3
CAST_ROWS = {D_MODEL: D_MODEL // N_TILES, D_FF: D_FF // (N_TILES // 2)}
VMEM_LIMIT = 56 * 1024 * 1024

F32 = jnp.float32
BF16 = jnp.bfloat16


def _dot(a, b):
    return jnp.dot(a, b, preferred_element_type=F32)


def _dot_nt(a, b):
    return lax.dot_general(a, b, (((1,), (1,)), ((), ())), preferred_element_type=F32)


def _dot_tn(a, b):
    return lax.dot_general(a, b, (((0,), (0,)), ((), ())), preferred_element_type=F32)


def _rms(x, w):
    y = x * lax.rsqrt(jnp.mean(x * x, axis=-1, keepdims=True) + NORM_EPS)
    return y * w


def _swap_halves(x):
    return pltpu.roll(x, x.shape[-1] // 2, 1)


def _cast_blocks(src_refs, dst_refs):
    for src, dst in zip(src_refs, dst_refs, strict=True):
        dst[...] = src[...].astype(BF16)


def _pre_units(rows, x_ref, n1_ref, wg_ref, wu_ref, wd_ref, n2_ref, win_ref, cos_ref, sin_ref,
               h_ref, act_ref, hmid_ref):
    nrow = rows.stop - rows.start
    xn = _rms(x_ref[rows, :], n1_ref[...]).astype(BF16)
    yield
    for c in range(D_FF // FF_CHUNK):
        sl = slice(c * FF_CHUNK, (c + 1) * FF_CHUNK)
        g = _dot(xn, wg_ref[:, sl])
        u = _dot(xn, wu_ref[:, sl])
        hmid_ref[rows, sl] = (jax.nn.silu(g) * u).astype(BF16)
        yield
    for c in range(D_MODEL // OUT_CHUNK):
        sl = slice(c * OUT_CHUNK, (c + 1) * OUT_CHUNK)
        h_ref[rows, sl] = x_ref[rows, sl] + 0.5 * _dot(hmid_ref[rows, :], wd_ref[:, sl])
        yield
    u = _rms(h_ref[rows, :], n2_ref[...]).astype(BF16)

    def proj(lo, hi):
        return _dot(u, win_ref[:, lo:hi])

    act_ref[rows, A_AQ:A_KD] = (proj(W_AQ, W_KV) * (HEAD_DIM ** -0.5)).astype(BF16)
    yield

    first = lax.broadcasted_iota(jnp.int32, (nrow // 2, KV_WIDTH), 1) < HEAD_DIM
    for part in range(2):
        r = slice(rows.start + part * (nrow // 2), rows.start + (part + 1) * (nrow // 2))
        kv = _dot(u[part * (nrow // 2):(part + 1) * (nrow // 2), :], win_ref[:, W_KV:W_RQ])
        for src, dst in ((0, A_KD), (KV_WIDTH, A_VD)):
            t = kv[:, src:src + KV_WIDTH]
            ts = _swap_halves(t)
            act_ref[r, dst:dst + KV_WIDTH] = jnp.where(first, t, ts).astype(BF16)
            act_ref[r, dst + KV_WIDTH:dst + 2 * KV_WIDTH] = jnp.where(first, ts, t).astype(BF16)
    yield

    pos0 = pl.multiple_of((pl.program_id(0) % TILES_PER_SEQ) * TM + rows.start, nrow)
    cos = cos_ref[pl.ds(pos0, nrow), :]
    sin = sin_ref[pl.ds(pos0, nrow), :]
    for w0, a0, scale in ((W_RQ, A_RQ, 1.0), (W_RK, A_RK, RET_HEAD_DIM ** -0.5)):
        z = proj(w0, w0 + RET_WIDTH)
        for hd in range(N_RET_HEADS):
            zh = z[:, hd * RET_HEAD_DIM:(hd + 1) * RET_HEAD_DIM]
            zr = zh * cos + _swap_halves(zh) * sin
            if scale != 1.0:
                zr = zr * scale
            act_ref[rows, a0 + hd * RET_HEAD_DIM:a0 + (hd + 1) * RET_HEAD_DIM] = zr.astype(BF16)
        yield
    act_ref[rows, A_RV:A_SG] = proj(W_RV, W_RG).astype(BF16)
    yield
    act_ref[rows, A_SG:D_ACT] = jax.nn.silu(proj(W_RG, D_IN)).astype(BF16)
    yield


def _delayed(gen, n):
    for _ in range(n):
        yield
    yield from gen


def _pre_kernel(x_ref, n1_ref, wg_ref, wu_ref, wd_ref, n2_ref, win_ref, cos_ref, sin_ref, *rest):
    cast_src, (h_ref, act_ref), cast_dst, hmid_ref = rest[:4], rest[4:6], rest[6:10], rest[10]
    _cast_blocks(cast_src, cast_dst)
    args = (x_ref, n1_ref, wg_ref, wu_ref, wd_ref, n2_ref, win_ref, cos_ref, sin_ref, h_ref, act_ref, hmid_ref)
    half = TM // 2
    _interleave((_pre_units(slice(0, half), *args), N_PRE_UNITS),
                (_delayed(_pre_units(slice(half, TM), *args), PRE_SKEW), N_PRE_UNITS + PRE_SKEW))


def _mix_units(seq_start, consumed, sink_ref, decay_ref, act_ref, kvp_ref, dmat_ref, zeta_ref, xi_ref, gnw_ref,
               o_ref, state_ref):
    def writable(j):
        return (j * BLK) // (TM // 2) in consumed

    row = lax.broadcasted_iota(jnp.int32, (BLK, 2 * BLK), 0)
    col = lax.broadcasted_iota(jnp.int32, (BLK, 2 * BLK), 1)
    band = (col > row) & (col <= row + WINDOW)
    band0 = band & ((col >= BLK) | jnp.logical_not(seq_start))
    lo = lax.broadcasted_iota(jnp.int32, (BLK, 2 * HEAD_DIM), 1) < HEAD_DIM
    zero_bf = jnp.zeros((BLK, 2 * HEAD_DIM), BF16)
    pair_w = 2 * HEAD_DIM

    def attention(j, kvh):
        r0 = j * BLK
        rows = slice(r0, r0 + BLK)
        valid = band0 if j == 0 else band
        kc = A_KD + kvh * pair_w
        vc = A_VD + kvh * pair_w
        if j == 0:
            k_h = jnp.concatenate([kvp_ref[:, kc - A_KD:kc - A_KD + pair_w], act_ref[0:BLK, kc:kc + pair_w]], axis=0)
            v_h = jnp.concatenate([kvp_ref[:, vc - A_KD:vc - A_KD + pair_w], act_ref[0:BLK, vc:vc + pair_w]], axis=0)
        else:
            k_h = act_ref[r0 - BLK:r0 + BLK, kc:kc + pair_w]
            v_h = act_ref[r0 - BLK:r0 + BLK, vc:vc + pair_w]
        qs = []
        for p in range(GQA_GROUP // 2):
            c0 = A_AQ + (kvh * 2 + p) * pair_w
            qp = act_ref[rows, c0:c0 + pair_w]
            qs.append(jnp.where(lo, qp, zero_bf))
            qs.append(jnp.where(lo, zero_bf, qp))
        s_all = _dot_nt(jnp.concatenate(qs, axis=0), k_h)
        yield
        ps, invs = [], []
        for g in range(GQA_GROUP):
            sink = sink_ref[kvh * GQA_GROUP + g]
            s = jnp.where(valid, s_all[g * BLK:(g + 1) * BLK, :], NEG_INF)
            m = jnp.maximum(jnp.max(s, axis=-1, keepdims=True), sink)
            e = jnp.exp(s - m)
            den = jnp.sum(e, axis=-1, keepdims=True) + jnp.exp(sink - m)
            ps.append(e.astype(BF16))
            invs.append(1.0 / den)
        o_all = _dot(jnp.concatenate(ps, axis=0), v_h)
        yield
        assert writable(j), "mixing output traced before the previous tile's rows were read"
        for p in range(GQA_GROUP // 2):
            c0 = (kvh * 2 + p) * pair_w
            oe = o_all[(2 * p) * BLK:(2 * p + 1) * BLK, :] * invs[2 * p]
            oo = o_all[(2 * p + 1) * BLK:(2 * p + 2) * BLK, :] * invs[2 * p + 1]
            o_ref[rows, c0:c0 + pair_w] = jnp.where(lo, oe, oo).astype(BF16)

    def retention(j, heads):
        rows = slice(j * BLK, (j + 1) * BLK)
        part = []
        for hd in heads:
            c = hd * RET_HEAD_DIM
            q = act_ref[rows, A_RQ + c:A_RQ + c + RET_HEAD_DIM]
            k = act_ref[rows, A_RK + c:A_RK + c + RET_HEAD_DIM]
            v = act_ref[rows, A_RV + c:A_RV + c + RET_HEAD_DIM]
            vz = (v.astype(F32) * zeta_ref[hd]).astype(BF16)
            part.append((v, _dot_nt(q, k), _dot(q, state_ref[hd].astype(BF16)), _dot_tn(k, vz)))
        yield
        intra = []
        for hd, (v, sc, _, kv) in zip(heads, part, strict=True):
            state_ref[hd] = state_ref[hd] * decay_ref[hd] + kv
            intra.append(_dot((sc * dmat_ref[hd]).astype(BF16), v))
        yield
        assert writable(j), "mixing output traced before the previous tile's rows were read"
        for hd, (_, _, cross, _), y_intra in zip(heads, part, intra, strict=True):
            c = hd * RET_HEAD_DIM
            y = y_intra + cross * xi_ref[hd]
            mu = jnp.mean(y, axis=-1, keepdims=True)
            d = y - mu
            var = jnp.mean(d * d, axis=-1, keepdims=True)
            yn = d * lax.rsqrt(var + GN_EPS) * gnw_ref[:, c:c + RET_HEAD_DIM]
            sg = act_ref[rows, A_SG + c:A_SG + c + RET_HEAD_DIM]
            o_ref[rows, ATTN_WIDTH + c:ATTN_WIDTH + c + RET_HEAD_DIM] = (sg.astype(F32) * yn).astype(BF16)

    chains = []
    for j in range(BLKS_PER_TILE):
        chains += [attention(j, 0), attention(j, 1), retention(j, (0, 1)), retention(j, (2, 3))]
    active = []
    for chain in chains + [None] * (MIX_STAGES - 1):
        if chain is not None:
            active.append(chain)
        for g in list(active):
            try:
                next(g)
            except StopIteration:
                active.remove(g)
        yield


def _post_units(final, rows, consumed, ar_ref, h_ref, wo_ref, n_ref, wg_ref, wu_ref, wd_ref, fn_ref, o_ref,
                hmid_ref):
    h2 = h_ref[rows, :] + _dot(ar_ref[rows, :], wo_ref[...])
    consumed.add(rows.start // (rows.stop - rows.start))
    o_ref[rows, :] = h2
    xn = _rms(h2, n_ref[...]).astype(BF16)
    yield
    for c in range(D_FF // FF_CHUNK):
        sl = slice(c * FF_CHUNK, (c + 1) * FF_CHUNK)
        g = _dot(xn, wg_ref[:, sl])
        u = _dot(xn, wu_ref[:, sl])
        hmid_ref[rows, sl] = (jax.nn.silu(g) * u).astype(BF16)
        yield
    for c in range(D_MODEL // OUT_CHUNK):
        sl = slice(c * OUT_CHUNK, (c + 1) * OUT_CHUNK)
        o_ref[rows, sl] = o_ref[rows, sl] + 0.5 * _dot(hmid_ref[rows, :], wd_ref[:, sl])
        yield
    if final:
        o_ref[rows, :] = _rms(o_ref[rows, :], fn_ref[...])


def _interleave(*streams, early=()):
    total = max(n for _, n in streams)
    done = [0] * len(streams)
    for k in range(1, total + 1):
        for idx, (gen, n) in enumerate(streams):
            span = total - EARLY_ROUNDS if idx in early else total
            while done[idx] < -(-min(k, span) * n // span):
                next(gen)
                done[idx] += 1
    for gen, _ in streams:
        for _ in gen:
            raise AssertionError("stream yielded more often than declared")


def _mixpost_kernel(final, sink_ref, decay_ref, act_ref, kvp_ref, dmat_ref, zeta_ref, xi_ref, gnw_ref,
                    h_ref, wo_ref, n_ref, wg_ref, wu_ref, wd_ref, fn_ref, *rest):
    n_cast = 0 if final else 4
    cast_src, o_ref, cast_dst = rest[:n_cast], rest[n_cast], rest[n_cast + 1:2 * n_cast + 1]
    state_ref, ar_ref, hmid_ref = rest[2 * n_cast + 1:]
    t = pl.program_id(0)
    seq_start = (jnp.minimum(t, N_TILES - 1) % TILES_PER_SEQ) == 0

    @pl.when(t == 0)
    def _():
        ar_ref[...] = jnp.zeros((TM, D_MODEL), BF16)

    @pl.when(seq_start)
    def _():
        state_ref[...] = jnp.zeros_like(state_ref)

    _cast_blocks(cast_src, cast_dst)
    consumed = set()
    post_args = (consumed, ar_ref, h_ref, wo_ref, n_ref, wg_ref, wu_ref, wd_ref, fn_ref, o_ref, hmid_ref)
    half = TM // 2
    _interleave(
        (_post_units(final, slice(0, half), *post_args), N_POST_UNITS),
        (_delayed(_post_units(final, slice(half, TM), *post_args), POST_SKEW), N_POST_UNITS + POST_SKEW),
        (_delayed(_mix_units(seq_start, consumed, sink_ref, decay_ref, act_ref, kvp_ref, dmat_ref, zeta_ref,
                             xi_ref, gnw_ref, ar_ref, state_ref), MIX_DELAY), N_MIX_SLOTS + MIX_DELAY),
        early=(2,))


def _resident(shape, layer):
    nd = len(shape)
    return pl.BlockSpec((None,) + shape, lambda *_: (layer,) + (0,) * nd, pipeline_mode=pl.Buffered(1))


def _const(shape):
    nd = len(shape)
    return pl.BlockSpec(shape, lambda *_: (0,) * nd, pipeline_mode=pl.Buffered(1))


def _cast_specs(layer, shapes):
    ins, outs, shapes_out = [], [], []
    for rows, cols in shapes:
        blk = CAST_ROWS[rows]
        last = rows // blk - 1
        ins.append(pl.BlockSpec((None, blk, cols), lambda i, last=last: (layer, jnp.minimum(i, last), 0)))
        outs.append(pl.BlockSpec((blk, cols), lambda i, last=last: (jnp.minimum(i, last), 0)))
        shapes_out.append(jax.ShapeDtypeStruct((rows, cols), BF16))
    return ins, outs, shapes_out


def _pre_call(layer, x, n1, wg, wu, wd, n2, win, cos_t, sin_t, next_f32):
    tok = lambda w: pl.BlockSpec((TM, w), lambda i: (i, 0))
    cast_in, cast_out, cast_shapes = _cast_specs(
        layer, ((D_MODEL, D_MODEL), (D_MODEL, D_FF), (D_MODEL, D_FF), (D_FF, D_MODEL)))
    return pl.pallas_call(
        _pre_kernel,
        grid=(N_TILES,),
        in_specs=[
            tok(D_MODEL),
            _resident((1, D_MODEL), layer),
            _const((D_MODEL, D_FF)),
            _const((D_MODEL, D_FF)),
            _const((D_FF, D_MODEL)),
            _resident((1, D_MODEL), layer),
            _const((D_MODEL, D_IN)),
            _const((SEQ, RET_HEAD_DIM)),
            _const((SEQ, RET_HEAD_DIM)),
        ] + cast_in,
        out_specs=[tok(D_MODEL), tok(D_ACT)] + cast_out,
        out_shape=[jax.ShapeDtypeStruct((N_TOK, D_MODEL), F32),
                   jax.ShapeDtypeStruct((N_TOK, D_ACT), BF16)] + cast_shapes,
        scratch_shapes=[pltpu.VMEM((TM, D_FF), BF16)],
        compiler_params=pltpu.CompilerParams(dimension_semantics=("arbitrary",), vmem_limit_bytes=VMEM_LIMIT),
        name=f"pre{layer}",
    )(x, n1, wg, wu, wd, n2, win, cos_t, sin_t, *next_f32)


def _mixpost_call(layer, final, sinks, decay, act, dmat, zeta_b, xi_b, gnw, h, wo, n, wg, wu, wd, fn, next_f32):
    def mix_tile(i):
        return jnp.minimum(i, N_TILES - 1)

    if final:
        cast_in, cast_out, cast_shapes = [], [], []
    else:
        cast_in, cast_out, cast_shapes = _cast_specs(
            layer + 1, ((D_MODEL, D_FF), (D_MODEL, D_FF), (D_FF, D_MODEL), (D_MODEL, D_IN)))

    prev_kv = pl.BlockSpec((BLK, A_RQ - A_KD),
                           lambda i: (jnp.maximum(mix_tile(i) * BLKS_PER_TILE - 1, 0), A_KD // (A_RQ - A_KD)))
    lag = pl.BlockSpec((TM, D_MODEL), lambda i: (jnp.maximum(i - 1, 0), 0))
    smem = pl.BlockSpec(memory_space=pltpu.SMEM)
    return pl.pallas_call(
        functools.partial(_mixpost_kernel, final),
        grid=(N_TILES + 1,),
        in_specs=[smem, smem,
                  pl.BlockSpec((TM, D_ACT), lambda i: (mix_tile(i), 0)),
                  prev_kv,
                  _const((N_RET_HEADS, BLK, BLK)), _const((N_RET_HEADS, BLK, BLK)), _const((N_RET_HEADS, BLK, BLK)),
                  _resident((1, RET_WIDTH), layer),
                  lag,
                  _const((D_MODEL, D_MODEL)),
                  _resident((1, D_MODEL), layer),
                  _const((D_MODEL, D_FF)),
                  _const((D_MODEL, D_FF)),
                  _const((D_FF, D_MODEL)),
                  _const((1, D_MODEL))] + cast_in,
        out_specs=[lag] + cast_out,
        out_shape=[jax.ShapeDtypeStruct((N_TOK, D_MODEL), F32)] + cast_shapes,
        scratch_shapes=[pltpu.VMEM((N_RET_HEADS, RET_HEAD_DIM, RET_HEAD_DIM), F32),
                        pltpu.VMEM((TM, D_MODEL), BF16),
                        pltpu.VMEM((TM, D_FF), BF16)],
        compiler_params=pltpu.CompilerParams(dimension_semantics=("arbitrary",), vmem_limit_bytes=VMEM_LIMIT),
        name=f"mixpost{layer}",
    )(sinks, decay, act, act, dmat, zeta_b, xi_b, gnw, h, wo, n, wg, wu, wd, fn, *next_f32)


def _tables():
    pos = jnp.arange(SEQ, dtype=F32)
    inv_freq = ROPE_BASE ** (-jnp.arange(0, RET_HEAD_DIM, 2, dtype=F32) / RET_HEAD_DIM)
    ang = pos[:, None] * inv_freq[None, :]
    cos, sin = jnp.cos(ang), jnp.sin(ang)
    cos_t = jnp.concatenate([cos, cos], axis=-1)
    sin_t = jnp.concatenate([-sin, sin], axis=-1)
    log_gamma = jnp.log(1.0 - 2.0 ** (-5.0 - jnp.arange(N_RET_HEADS, dtype=F32)))
    idx = jnp.arange(BLK, dtype=F32)
    dif = idx[:, None] - idx[None, :]
    dmat = jnp.where(dif[None] >= 0, jnp.exp(jnp.maximum(dif, 0.0)[None] * log_gamma[:, None, None]), 0.0)
    zeta = jnp.exp((BLK - 1.0 - idx)[None, :] * log_gamma[:, None])
    xi = jnp.exp((idx + 1.0)[None, :] * log_gamma[:, None])
    decay = jnp.exp(BLK * log_gamma)
    bshape = (N_RET_HEADS, BLK, RET_HEAD_DIM)
    zeta_b = jnp.broadcast_to(zeta[:, :, None], bshape)
    xi_b = jnp.broadcast_to(xi[:, :, None], bshape)
    return cos_t, sin_t, dmat, zeta_b, xi_b, decay


def kernel(x, ffn1_norm, ffn1_w_gate, ffn1_w_up, ffn1_w_down, mix_norm, w_in, attn_sinks, ret_gn_w, w_out,
           ffn2_norm, ffn2_w_gate, ffn2_w_up, ffn2_w_down, final_norm):
    cos_t, sin_t, dmat, zeta_b, xi_b, decay = _tables()
    pre_w = [w[0].astype(BF16) for w in (ffn1_w_gate, ffn1_w_up, ffn1_w_down, w_in)]
    pre_f32 = (ffn1_w_gate, ffn1_w_up, ffn1_w_down, w_in)
    post_f32 = (w_out, ffn2_w_gate, ffn2_w_up, ffn2_w_down)
    n1 = ffn1_norm.reshape(DEPTH, 1, D_MODEL)
    nm = mix_norm.reshape(DEPTH, 1, D_MODEL)
    n2 = ffn2_norm.reshape(DEPTH, 1, D_MODEL)
    gnw = ret_gn_w.reshape(DEPTH, 1, RET_WIDTH)
    fn = final_norm.reshape(1, D_MODEL)

    h = x.reshape(N_TOK, D_MODEL)
    for layer in range(DEPTH):
        final = layer == DEPTH - 1
        wg1, wu1, wd1, win = pre_w
        h, act, wo, wg2, wu2, wd2 = _pre_call(layer, h, n1, wg1, wu1, wd1, nm, win, cos_t, sin_t, post_f32)
        h, *pre_w = _mixpost_call(layer, final, attn_sinks[layer], decay, act, dmat, zeta_b, xi_b, gnw,
                                  h, wo, n2, wg2, wu2, wd2, fn, () if final else pre_f32)
    return h.reshape(BATCH, SEQ, D_MODEL)
```

```python
import functools

import jax
import jax.numpy as jnp
from jax import lax
from jax.experimental import pallas as pl
from jax.experimental.pallas import tpu as pltpu

D_MODEL = 1024
BATCH = 8
SEQ = 2048
DEPTH = 4
ATTN_WIDTH = 512
RET_WIDTH = 512
HEAD_DIM = 64
N_KV_HEADS = 2
GQA_GROUP = 4
KV_WIDTH = 128
WINDOW = 128
BLK = 128
N_RET_HEADS = 4
RET_HEAD_DIM = 128
ROPE_BASE = 10000.0
D_FF = 2816
D_IN = 2816
NORM_EPS = 1e-6
GN_EPS = 1e-5
NEG_INF = -1e30
N_TOK = BATCH * SEQ

W_AQ, W_KV, W_RQ, W_RK, W_RV, W_RG = 0, 512, 768, 1280, 1792, 2304
A_AQ, A_KD, A_VD, A_RQ, A_RK, A_RV, A_SG, D_ACT = 0, 512, 768, 1024, 1536, 2048, 2560, 3072

TM = 512
N_TILES = N_TOK // TM
TILES_PER_SEQ = SEQ // TM
BLKS_PER_TILE = TM // BLK
FF_CHUNK = 256
OUT_CHUNK = 512
N_PRE_UNITS = 1 + D_FF // FF_CHUNK + D_MODEL // OUT_CHUNK + 6
PRE_SKEW = 3
MIX_STAGES = 3
N_MIX_SLOTS = 4 * BLKS_PER_TILE + MIX_STAGES - 1
N_POST_UNITS = 1 + D_FF // FF_CHUNK + D_MODEL // OUT_CHUNK
POST_SKEW = 2
MIX_DELAY = 1
EARLY_ROUNDS = 3
CAST_ROWS = {D_MODEL: D_MODEL // N_TILES, D_FF: D_FF // (N_TILES // 2)}
VMEM_LIMIT = 56 * 1024 * 1024

F32 = jnp.float32
BF16 = jnp.bfloat16


def _dot(a, b):
    return jnp.dot(a, b, preferred_element_type=F32)


def _dot_nt(a, b):
    return lax.dot_general(a, b, (((1,), (1,)), ((), ())), preferred_element_type=F32)


def _dot_tn(a, b):
    return lax.dot_general(a, b, (((0,), (0,)), ((), ())), preferred_element_type=F32)


def _rms(x, w):
    y = x * lax.rsqrt(jnp.mean(x * x, axis=-1, keepdims=True) + NORM_EPS)
    return y * w


def _swap_halves(x):
    return pltpu.roll(x, x.shape[-1] // 2, 1)


def _cast_blocks(src_refs, dst_refs):
    for src, dst in zip(src_refs, dst_refs, strict=True):
        dst[...] = src[...].astype(BF16)


def _pre_units(rows, x_ref, n1_ref, wg_ref, wu_ref, wd_ref, n2_ref, win_ref, cos_ref, sin_ref,
               h_ref, act_ref, hmid_ref):
    nrow = rows.stop - rows.start
    xn = _rms(x_ref[rows, :], n1_ref[...]).astype(BF16)
    yield
    for c in range(D_FF // FF_CHUNK):
        sl = slice(c * FF_CHUNK, (c + 1) * FF_CHUNK)
        g = _dot(xn, wg_ref[:, sl])
        u = _dot(xn, wu_ref[:, sl])
        hmid_ref[rows, sl] = (jax.nn.silu(g) * u).astype(BF16)
        yield
    for c in range(D_MODEL // OUT_CHUNK):
        sl = slice(c * OUT_CHUNK, (c + 1) * OUT_CHUNK)
        h_ref[rows, sl] = x_ref[rows, sl] + 0.5 * _dot(hmid_ref[rows, :], wd_ref[:, sl])
        yield
    u = _rms(h_ref[rows, :], n2_ref[...]).astype(BF16)

    def proj(lo, hi):
        return _dot(u, win_ref[:, lo:hi])

    act_ref[rows, A_AQ:A_KD] = (proj(W_AQ, W_KV) * (HEAD_DIM ** -0.5)).astype(BF16)
    yield

    first = lax.broadcasted_iota(jnp.int32, (nrow // 2, KV_WIDTH), 1) < HEAD_DIM
    for part in range(2):
        r = slice(rows.start + part * (nrow // 2), rows.start + (part + 1) * (nrow // 2))
        kv = _dot(u[part * (nrow // 2):(part + 1) * (nrow // 2), :], win_ref[:, W_KV:W_RQ])
        for src, dst in ((0, A_KD), (KV_WIDTH, A_VD)):
            t = kv[:, src:src + KV_WIDTH]
            ts = _swap_halves(t)
            act_ref[r, dst:dst + KV_WIDTH] = jnp.where(first, t, ts).astype(BF16)
            act_ref[r, dst + KV_WIDTH:dst + 2 * KV_WIDTH] = jnp.where(first, ts, t).astype(BF16)
    yield

    pos0 = pl.multiple_of((pl.program_id(0) % TILES_PER_SEQ) * TM + rows.start, nrow)
    cos = cos_ref[pl.ds(pos0, nrow), :]
    sin = sin_ref[pl.ds(pos0, nrow), :]
    for w0, a0, scale in ((W_RQ, A_RQ, 1.0), (W_RK, A_RK, RET_HEAD_DIM ** -0.5)):
        z = proj(w0, w0 + RET_WIDTH)
        for hd in range(N_RET_HEADS):
            zh = z[:, hd * RET_HEAD_DIM:(hd + 1) * RET_HEAD_DIM]
            zr = zh * cos + _swap_halves(zh) * sin
            if scale != 1.0:
                zr = zr * scale
            act_ref[rows, a0 + hd * RET_HEAD_DIM:a0 + (hd + 1) * RET_HEAD_DIM] = zr.astype(BF16)
        yield
    act_ref[rows, A_RV:A_SG] = proj(W_RV, W_RG).astype(BF16)
    yield
    act_ref[rows, A_SG:D_ACT] = jax.nn.silu(proj(W_RG, D_IN)).astype(BF16)
    yield


def _delayed(gen, n):
    for _ in range(n):
        yield
    yield from gen


def _pre_kernel(x_ref, n1_ref, wg_ref, wu_ref, wd_ref, n2_ref, win_ref, cos_ref, sin_ref, *rest):
    cast_src, (h_ref, act_ref), cast_dst, hmid_ref = rest[:4], rest[4:6], rest[6:10], rest[10]
    _cast_blocks(cast_src, cast_dst)
    args = (x_ref, n1_ref, wg_ref, wu_ref, wd_ref, n2_ref, win_ref, cos_ref, sin_ref, h_ref, act_ref, hmid_ref)
    half = TM // 2
    _interleave((_pre_units(slice(0, half), *args), N_PRE_UNITS),
                (_delayed(_pre_units(slice(half, TM), *args), PRE_SKEW), N_PRE_UNITS + PRE_SKEW))


def _mix_units(seq_start, consumed, sink_ref, decay_ref, act_ref, kvp_ref, dmat_ref, zeta_ref, xi_ref, gnw_ref,
               o_ref, state_ref):
    def writable(j):
        return (j * BLK) // (TM // 2) in consumed

    row = lax.broadcasted_iota(jnp.int32, (BLK, 2 * BLK), 0)
    col = lax.broadcasted_iota(jnp.int32, (BLK, 2 * BLK), 1)
    band = (col > row) & (col <= row + WINDOW)
    band0 = band & ((col >= BLK) | jnp.logical_not(seq_start))
    lo = lax.broadcasted_iota(jnp.int32, (BLK, 2 * HEAD_DIM), 1) < HEAD_DIM
    zero_bf = jnp.zeros((BLK, 2 * HEAD_DIM), BF16)
    pair_w = 2 * HEAD_DIM

    def attention(j, kvh):
        r0 = j * BLK
        rows = slice(r0, r0 + BLK)
        valid = band0 if j == 0 else band
        kc = A_KD + kvh * pair_w
        vc = A_VD + kvh * pair_w
        if j == 0:
            k_h = jnp.concatenate([kvp_ref[:, kc - A_KD:kc - A_KD + pair_w], act_ref[0:BLK, kc:kc + pair_w]], axis=0)
            v_h = jnp.concatenate([kvp_ref[:, vc - A_KD:vc - A_KD + pair_w], act_ref[0:BLK, vc:vc + pair_w]], axis=0)
        else:
            k_h = act_ref[r0 - BLK:r0 + BLK, kc:kc + pair_w]
            v_h = act_ref[r0 - BLK:r0 + BLK, vc:vc + pair_w]
        qs = []
        for p in range(GQA_GROUP // 2):
            c0 = A_AQ + (kvh * 2 + p) * pair_w
            qp = act_ref[rows, c0:c0 + pair_w]
            qs.append(jnp.where(lo, qp, zero_bf))
            qs.append(jnp.where(lo, zero_bf, qp))
        s_all = _dot_nt(jnp.concatenate(qs, axis=0), k_h)
        yield
        ps, invs = [], []
        for g in range(GQA_GROUP):
            sink = sink_ref[kvh * GQA_GROUP + g]
            s = jnp.where(valid, s_all[g * BLK:(g + 1) * BLK, :], NEG_INF)
            m = jnp.maximum(jnp.max(s, axis=-1, keepdims=True), sink)
            e = jnp.exp(s - m)
            den = jnp.sum(e, axis=-1, keepdims=True) + jnp.exp(sink - m)
            ps.append(e.astype(BF16))
            invs.append(1.0 / den)
        o_all = _dot(jnp.concatenate(ps, axis=0), v_h)
        yield
        assert writable(j), "mixing output traced before the previous tile's rows were read"
        for p in range(GQA_GROUP // 2):
            c0 = (kvh * 2 + p) * pair_w
            oe = o_all[(2 * p) * BLK:(2 * p + 1) * BLK, :] * invs[2 * p]
            oo = o_all[(2 * p + 1) * BLK:(2 * p + 2) * BLK, :] * invs[2 * p + 1]
            o_ref[rows, c0:c0 + pair_w] = jnp.where(lo, oe, oo).astype(BF16)

    def retention(j, heads):
        rows = slice(j * BLK, (j + 1) * BLK)
        part = []
        for hd in heads:
            c = hd * RET_HEAD_DIM
            q = act_ref[rows, A_RQ + c:A_RQ + c + RET_HEAD_DIM]
            k = act_ref[rows, A_RK + c:A_RK + c + RET_HEAD_DIM]
            v = act_ref[rows, A_RV + c:A_RV + c + RET_HEAD_DIM]
            vz = (v.astype(F32) * zeta_ref[hd]).astype(BF16)
            part.append((v, _dot_nt(q, k), _dot(q, state_ref[hd].astype(BF16)), _dot_tn(k, vz)))
        yield
        intra = []
        for hd, (v, sc, _, kv) in zip(heads, part, strict=True):
            state_ref[hd] = state_ref[hd] * decay_ref[hd] + kv
            intra.append(_dot((sc * dmat_ref[hd]).astype(BF16), v))
        yield
        assert writable(j), "mixing output traced before the previous tile's rows were read"
        for hd, (_, _, cross, _), y_intra in zip(heads, part, intra, strict=True):
            c = hd * RET_HEAD_DIM
            y = y_intra + cross * xi_ref[hd]
            mu = jnp.mean(y, axis=-1, keepdims=True)
            d = y - mu
            var = jnp.mean(d * d, axis=-1, keepdims=True)
            yn = d * lax.rsqrt(var + GN_EPS) * gnw_ref[:, c:c + RET_HEAD_DIM]
            sg = act_ref[rows, A_SG + c:A_SG + c + RET_HEAD_DIM]
            o_ref[rows, ATTN_WIDTH + c:ATTN_WIDTH + c + RET_HEAD_DIM] = (sg.astype(F32) * yn).astype(BF16)

    chains = []
    for j in range(BLKS_PER_TILE):
        chains += [attention(j, 0), attention(j, 1), retention(j, (0, 1)), retention(j, (2, 3))]
    active = []
    for chain in chains + [None] * (MIX_STAGES - 1):
        if chain is not None:
            active.append(chain)
        for g in list(active):
            try:
                next(g)
            except StopIteration:
                active.remove(g)
        yield


def _post_units(final, rows, consumed, ar_ref, h_ref, wo_ref, n_ref, wg_ref, wu_ref, wd_ref, fn_ref, o_ref,
                hmid_ref):
    h2 = h_ref[rows, :] + _dot(ar_ref[rows, :], wo_ref[...])
    consumed.add(rows.start // (rows.stop - rows.start))
    o_ref[rows, :] = h2
    xn = _rms(h2, n_ref[...]).astype(BF16)
    yield
    for c in range(D_FF // FF_CHUNK):
        sl = slice(c * FF_CHUNK, (c + 1) * FF_CHUNK)
        g = _dot(xn, wg_ref[:, sl])
        u = _dot(xn, wu_ref[:, sl])
        hmid_ref[rows, sl] = (jax.nn.silu(g) * u).astype(BF16)
        yield
    for c in range(D_MODEL // OUT_CHUNK):
        sl = slice(c * OUT_CHUNK, (c + 1) * OUT_CHUNK)
        o_ref[rows, sl] = o_ref[rows, sl] + 0.5 * _dot(hmid_ref[rows, :], wd_ref[:, sl])
        yield
    if final:
        o_ref[rows, :] = _rms(o_ref[rows, :], fn_ref[...])


def _interleave(*streams, early=()):
    total = max(n for _, n in streams)
    done = [0] * len(streams)
    for k in range(1, total + 1):
        for idx, (gen, n) in enumerate(streams):
            span = total - EARLY_ROUNDS if idx in early else total
            while done[idx] < -(-min(k, span) * n // span):
                next(gen)
                done[idx] += 1
    for gen, _ in streams:
        for _ in gen:
            raise AssertionError("stream yielded more often than declared")


def _mixpost_kernel(final, sink_ref, decay_ref, act_ref, kvp_ref, dmat_ref, zeta_ref, xi_ref, gnw_ref,
                    h_ref, wo_ref, n_ref, wg_ref, wu_ref, wd_ref, fn_ref, *rest):
    n_cast = 0 if final else 4
    cast_src, o_ref, cast_dst = rest[:n_cast], rest[n_cast], rest[n_cast + 1:2 * n_cast + 1]
    state_ref, ar_ref, hmid_ref = rest[2 * n_cast + 1:]
    t = pl.program_id(0)
    seq_start = (jnp.minimum(t, N_TILES - 1) % TILES_PER_SEQ) == 0

    @pl.when(seq_start)
    def _():
        state_ref[...] = jnp.zeros_like(state_ref)

    half = TM // 2

    def post_streams(consumed):
        args = (consumed, ar_ref, h_ref, wo_ref, n_ref, wg_ref, wu_ref, wd_ref, fn_ref, o_ref, hmid_ref)
        return ((_post_units(final, slice(0, half), *args), N_POST_UNITS),
                (_delayed(_post_units(final, slice(half, TM), *args), POST_SKEW), N_POST_UNITS + POST_SKEW))

    def mix_stream(consumed):
        return (_delayed(_mix_units(seq_start, consumed, sink_ref, decay_ref, act_ref, kvp_ref, dmat_ref, zeta_ref,
                                    xi_ref, gnw_ref, ar_ref, state_ref), MIX_DELAY), N_MIX_SLOTS + MIX_DELAY)

    @pl.when(t == 0)
    def _():
        _cast_blocks(cast_src, cast_dst)
        _interleave(mix_stream({0, 1}))

    @pl.when((t > 0) & (t < N_TILES))
    def _():
        _cast_blocks(cast_src, cast_dst)
        consumed = set()
        _interleave(*post_streams(consumed), mix_stream(consumed), early=(2,))

    @pl.when(t == N_TILES)
    def _():
        _cast_blocks(cast_src, cast_dst)
        _interleave(*post_streams(set()))


def _resident(shape, layer):
    nd = len(shape)
    return pl.BlockSpec((None,) + shape, lambda *_: (layer,) + (0,) * nd, pipeline_mode=pl.Buffered(1))


def _const(shape):
    nd = len(shape)
    return pl.BlockSpec(shape, lambda *_: (0,) * nd, pipeline_mode=pl.Buffered(1))


def _cast_specs(layer, shapes):
    ins, outs, shapes_out = [], [], []
    for rows, cols in shapes:
        blk = CAST_ROWS[rows]
        last = rows // blk - 1
        ins.append(pl.BlockSpec((None, blk, cols), lambda i, last=last: (layer, jnp.minimum(i, last), 0)))
        outs.append(pl.BlockSpec((blk, cols), lambda i, last=last: (jnp.minimum(i, last), 0)))
        shapes_out.append(jax.ShapeDtypeStruct((rows, cols), BF16))
    return ins, outs, shapes_out


def _pre_call(layer, x, n1, wg, wu, wd, n2, win, cos_t, sin_t, next_f32):
    tok = lambda w: pl.BlockSpec((TM, w), lambda i: (i, 0))
    cast_in, cast_out, cast_shapes = _cast_specs(
        layer, ((D_MODEL, D_MODEL), (D_MODEL, D_FF), (D_MODEL, D_FF), (D_FF, D_MODEL)))
    return pl.pallas_call(
        _pre_kernel,
        grid=(N_TILES,),
        in_specs=[
            tok(D_MODEL),
            _resident((1, D_MODEL), layer),
            _const((D_MODEL, D_FF)),
            _const((D_MODEL, D_FF)),
            _const((D_FF, D_MODEL)),
            _resident((1, D_MODEL), layer),
            _const((D_MODEL, D_IN)),
            _const((SEQ, RET_HEAD_DIM)),
            _const((SEQ, RET_HEAD_DIM)),
        ] + cast_in,
        out_specs=[tok(D_MODEL), tok(D_ACT)] + cast_out,
        out_shape=[jax.ShapeDtypeStruct((N_TOK, D_MODEL), F32),
                   jax.ShapeDtypeStruct((N_TOK, D_ACT), BF16)] + cast_shapes,
        scratch_shapes=[pltpu.VMEM((TM, D_FF), BF16)],
        compiler_params=pltpu.CompilerParams(dimension_semantics=("arbitrary",), vmem_limit_bytes=VMEM_LIMIT),
        name=f"pre{layer}",
    )(x, n1, wg, wu, wd, n2, win, cos_t, sin_t, *next_f32)


def _mixpost_call(layer, final, sinks, decay, act, dmat, zeta_b, xi_b, gnw, h, wo, n, wg, wu, wd, fn, next_f32):
    def mix_tile(i):
        return jnp.minimum(i, N_TILES - 1)

    if final:
        cast_in, cast_out, cast_shapes = [], [], []
    else:
        cast_in, cast_out, cast_shapes = _cast_specs(
            layer + 1, ((D_MODEL, D_FF), (D_MODEL, D_FF), (D_FF, D_MODEL), (D_MODEL, D_IN)))

    prev_kv = pl.BlockSpec((BLK, A_RQ - A_KD),
                           lambda i: (jnp.maximum(mix_tile(i) * BLKS_PER_TILE - 1, 0), A_KD // (A_RQ - A_KD)))
    lag = pl.BlockSpec((TM, D_MODEL), lambda i: (jnp.maximum(i - 1, 0), 0))
    smem = pl.BlockSpec(memory_space=pltpu.SMEM)
    return pl.pallas_call(
        functools.partial(_mixpost_kernel, final),
        grid=(N_TILES + 1,),
        in_specs=[smem, smem,
                  pl.BlockSpec((TM, D_ACT), lambda i: (mix_tile(i), 0)),
                  prev_kv,
                  _const((N_RET_HEADS, BLK, BLK)), _const((N_RET_HEADS, BLK, BLK)), _const((N_RET_HEADS, BLK, BLK)),
                  _resident((1, RET_WIDTH), layer),
                  lag,
                  _const((D_MODEL, D_MODEL)),
                  _resident((1, D_MODEL), layer),
                  _const((D_MODEL, D_FF)),
                  _const((D_MODEL, D_FF)),
                  _const((D_FF, D_MODEL)),
                  _const((1, D_MODEL))] + cast_in,
        out_specs=[lag] + cast_out,
        out_shape=[jax.ShapeDtypeStruct((N_TOK, D_MODEL), F32)] + cast_shapes,
        scratch_shapes=[pltpu.VMEM((N_RET_HEADS, RET_HEAD_DIM, RET_HEAD_DIM), F32),
                        pltpu.VMEM((TM, D_MODEL), BF16),
                        pltpu.VMEM((TM, D_FF), BF16)],
        compiler_params=pltpu.CompilerParams(dimension_semantics=("arbitrary",), vmem_limit_bytes=VMEM_LIMIT),
        name=f"mixpost{layer}",
    )(sinks, decay, act, act, dmat, zeta_b, xi_b, gnw, h, wo, n, wg, wu, wd, fn, *next_f32)


def _tables():
    pos = jnp.arange(SEQ, dtype=F32)
    inv_freq = ROPE_BASE ** (-jnp.arange(0, RET_HEAD_DIM, 2, dtype=F32) / RET_HEAD_DIM)
    ang = pos[:, None] * inv_freq[None, :]
    cos, sin = jnp.cos(ang), jnp.sin(ang)
    cos_t = jnp.concatenate([cos, cos], axis=-1)
    sin_t = jnp.concatenate([-sin, sin], axis=-1)
    log_gamma = jnp.log(1.0 - 2.0 ** (-5.0 - jnp.arange(N_RET_HEADS, dtype=F32)))
    idx = jnp.arange(BLK, dtype=F32)
    dif = idx[:, None] - idx[None, :]
    dmat = jnp.where(dif[None] >= 0, jnp.exp(jnp.maximum(dif, 0.0)[None] * log_gamma[:, None, None]), 0.0)
    zeta = jnp.exp((BLK - 1.0 - idx)[None, :] * log_gamma[:, None])
    xi = jnp.exp((idx + 1.0)[None, :] * log_gamma[:, None])
    decay = jnp.exp(BLK * log_gamma)
    bshape = (N_RET_HEADS, BLK, RET_HEAD_DIM)
    zeta_b = jnp.broadcast_to(zeta[:, :, None], bshape)
    xi_b = jnp.broadcast_to(xi[:, :, None], bshape)
    return cos_t, sin_t, dmat, zeta_b, xi_b, decay


def kernel(x, ffn1_norm, ffn1_w_gate, ffn1_w_up, ffn1_w_down, mix_norm, w_in, attn_sinks, ret_gn_w, w_out,
           ffn2_norm, ffn2_w_gate, ffn2_w_up, ffn2_w_down, final_norm):
    cos_t, sin_t, dmat, zeta_b, xi_b, decay = _tables()
    pre_w = [w[0].astype(BF16) for w in (ffn1_w_gate, ffn1_w_up, ffn1_w_down, w_in)]
    pre_f32 = (ffn1_w_gate, ffn1_w_up, ffn1_w_down, w_in)
    post_f32 = (w_out, ffn2_w_gate, ffn2_w_up, ffn2_w_down)
    n1 = ffn1_norm.reshape(DEPTH, 1, D_MODEL)
    nm = mix_norm.reshape(DEPTH, 1, D_MODEL)
    n2 = ffn2_norm.reshape(DEPTH, 1, D_MODEL)
    gnw = ret_gn_w.reshape(DEPTH, 1, RET_WIDTH)
    fn = final_norm.reshape(1, D_MODEL)

    h = x.reshape(N_TOK, D_MODEL)
    for layer in range(DEPTH):
        final = layer == DEPTH - 1
        wg1, wu1, wd1, win = pre_w
        h, act, wo, wg2, wu2, wd2 = _pre_call(layer, h, n1, wg1, wu1, wd1, nm, win, cos_t, sin_t, post_f32)
        h, *pre_w = _mixpost_call(layer, final, attn_sinks[layer], decay, act, dmat, zeta_b, xi_b, gnw,
                                  h, wo, n2, wg2, wu2, wd2, fn, () if final else pre_f32)
    return h.reshape(BATCH, SEQ, D_MODEL)
```

```python
import functools

import jax
import jax.numpy as jnp
from jax import lax
from jax.experimental import pallas as pl
from jax.experimental.pallas import tpu as pltpu

D_MODEL = 1024
BATCH = 8
SEQ = 2048
DEPTH = 4
ATTN_WIDTH = 512
RET_WIDTH = 512
HEAD_DIM = 64
N_KV_HEADS = 2
GQA_GROUP = 4
KV_WIDTH = 128
WINDOW = 128
BLK = 128
N_RET_HEADS = 4
RET_HEAD_DIM = 128
ROPE_BASE = 10000.0
D_FF = 2816
D_IN = 2816
NORM_EPS = 1e-6
GN_EPS = 1e-5
NEG_INF = -1e30
N_TOK = BATCH * SEQ

W_AQ, W_KV, W_RQ, W_RK, W_RV, W_RG = 0, 512, 768, 1280, 1792, 2304
A_AQ, A_KD, A_VD, A_RQ, A_RK, A_RV, A_SG, D_ACT = 0, 512, 768, 1024, 1536, 2048, 2560, 3072

TM = 512
N_TILES = N_TOK // TM
TILES_PER_SEQ = SEQ // TM
BLKS_PER_TILE = TM // BLK
FF_CHUNK = 256
OUT_CHUNK = 512
N_PRE_UNITS = 1 + D_FF // FF_CHUNK + D_MODEL // OUT_CHUNK + 6
PRE_SKEW = 3
MIX_STAGES = 3
N_MIX_SLOTS = 4 * BLKS_PER_TILE + MIX_STAGES - 1
N_POST_UNITS = 1 + D_FF // FF_CHUNK + D_MODEL // OUT_CHUNK
POST_SKEW = 2
MIX_DELAY = 1
EARLY_ROUNDS = 3
CAST_ROWS = {D_MODEL: D_MODEL // N_TILES, D_FF: D_FF // (N_TILES // 2)}
VMEM_LIMIT = 56 * 1024 * 1024

F32 = jnp.float32
BF16 = jnp.bfloat16


def _dot(a, b):
    return jnp.dot(a, b, preferred_element_type=F32)


def _dot_nt(a, b):
    return lax.dot_general(a, b, (((1,), (1,)), ((), ())), preferred_element_type=F32)


def _dot_tn(a, b):
    return lax.dot_general(a, b, (((0,), (0,)), ((), ())), preferred_element_type=F32)


def _rms(x, w):
    y = x * lax.rsqrt(jnp.mean(x * x, axis=-1, keepdims=True) + NORM_EPS)
    return y * w


def _swap_halves(x):
    return pltpu.roll(x, x.shape[-1] // 2, 1)


def _cast_blocks(src_refs, dst_refs):
    for src, dst in zip(src_refs, dst_refs, strict=True):
        dst[...] = src[...].astype(BF16)


def _pre_units(rows, x_ref, n1_ref, wg_ref, wu_ref, wd_ref, n2_ref, win_ref, cos_ref, sin_ref,
               h_ref, act_ref, hmid_ref):
    nrow = rows.stop - rows.start
    x = x_ref[rows, :]
    xw = (x * n1_ref[...]).astype(BF16)
    r = jnp.broadcast_to(lax.rsqrt(jnp.mean(x * x, axis=-1, keepdims=True) + NORM_EPS), (nrow, FF_CHUNK))
    yield
    for c in range(D_FF // FF_CHUNK):
        sl = slice(c * FF_CHUNK, (c + 1) * FF_CHUNK)
        g = _dot(xw, wg_ref[:, sl]) * r
        u = _dot(xw, wu_ref[:, sl]) * r
        hmid_ref[rows, sl] = (jax.nn.silu(g) * u).astype(BF16)
        yield
    for c in range(D_MODEL // OUT_CHUNK):
        sl = slice(c * OUT_CHUNK, (c + 1) * OUT_CHUNK)
        h_ref[rows, sl] = x_ref[rows, sl] + 0.5 * _dot(hmid_ref[rows, :], wd_ref[:, sl])
        yield
    u = _rms(h_ref[rows, :], n2_ref[...]).astype(BF16)

    def proj(lo, hi):
        return _dot(u, win_ref[:, lo:hi])

    act_ref[rows, A_AQ:A_KD] = (proj(W_AQ, W_KV) * (HEAD_DIM ** -0.5)).astype(BF16)
    yield

    first = lax.broadcasted_iota(jnp.int32, (nrow // 2, KV_WIDTH), 1) < HEAD_DIM
    for part in range(2):
        r = slice(rows.start + part * (nrow // 2), rows.start + (part + 1) * (nrow // 2))
        kv = _dot(u[part * (nrow // 2):(part + 1) * (nrow // 2), :], win_ref[:, W_KV:W_RQ])
        for src, dst in ((0, A_KD), (KV_WIDTH, A_VD)):
            t = kv[:, src:src + KV_WIDTH]
            ts = _swap_halves(t)
            act_ref[r, dst:dst + KV_WIDTH] = jnp.where(first, t, ts).astype(BF16)
            act_ref[r, dst + KV_WIDTH:dst + 2 * KV_WIDTH] = jnp.where(first, ts, t).astype(BF16)
    yield

    pos0 = pl.multiple_of((pl.program_id(0) % TILES_PER_SEQ) * TM + rows.start, nrow)
    cos = cos_ref[pl.ds(pos0, nrow), :]
    sin = sin_ref[pl.ds(pos0, nrow), :]
    for w0, a0, scale in ((W_RQ, A_RQ, 1.0), (W_RK, A_RK, RET_HEAD_DIM ** -0.5)):
        z = proj(w0, w0 + RET_WIDTH)
        for hd in range(N_RET_HEADS):
            zh = z[:, hd * RET_HEAD_DIM:(hd + 1) * RET_HEAD_DIM]
            zr = zh * cos + _swap_halves(zh) * sin
            if scale != 1.0:
                zr = zr * scale
            act_ref[rows, a0 + hd * RET_HEAD_DIM:a0 + (hd + 1) * RET_HEAD_DIM] = zr.astype(BF16)
        yield
    act_ref[rows, A_SG:D_ACT] = jax.nn.silu(proj(W_RG, D_IN)).astype(BF16)
    yield
    act_ref[rows, A_RV:A_SG] = proj(W_RV, W_RG).astype(BF16)
    yield


def _delayed(gen, n):
    for _ in range(n):
        yield
    yield from gen


def _pre_kernel(x_ref, n1_ref, wg_ref, wu_ref, wd_ref, n2_ref, win_ref, cos_ref, sin_ref, *rest):
    cast_src, (h_ref, act_ref), cast_dst, hmid_ref = rest[:4], rest[4:6], rest[6:10], rest[10]
    _cast_blocks(cast_src, cast_dst)
    args = (x_ref, n1_ref, wg_ref, wu_ref, wd_ref, n2_ref, win_ref, cos_ref, sin_ref, h_ref, act_ref, hmid_ref)
    half = TM // 2
    _interleave((_pre_units(slice(0, half), *args), N_PRE_UNITS),
                (_delayed(_pre_units(slice(half, TM), *args), PRE_SKEW), N_PRE_UNITS + PRE_SKEW))


def _mix_units(seq_start, consumed, sink_ref, decay_ref, act_ref, kvp_ref, dmat_ref, zeta_ref, xi_ref, gnw_ref,
               o_ref, state_ref):
    def writable(j):
        return (j * BLK) // (TM // 2) in consumed

    row = lax.broadcasted_iota(jnp.int32, (BLK, 2 * BLK), 0)
    col = lax.broadcasted_iota(jnp.int32, (BLK, 2 * BLK), 1)
    band = (col > row) & (col <= row + WINDOW)
    band0 = band & ((col >= BLK) | jnp.logical_not(seq_start))
    lo = lax.broadcasted_iota(jnp.int32, (BLK, 2 * HEAD_DIM), 1) < HEAD_DIM
    zero_bf = jnp.zeros((BLK, 2 * HEAD_DIM), BF16)
    pair_w = 2 * HEAD_DIM

    def attention(j, kvh):
        r0 = j * BLK
        rows = slice(r0, r0 + BLK)
        valid = band0 if j == 0 else band
        kc = A_KD + kvh * pair_w
        vc = A_VD + kvh * pair_w
        if j == 0:
            k_h = jnp.concatenate([kvp_ref[:, kc - A_KD:kc - A_KD + pair_w], act_ref[0:BLK, kc:kc + pair_w]], axis=0)
            v_h = jnp.concatenate([kvp_ref[:, vc - A_KD:vc - A_KD + pair_w], act_ref[0:BLK, vc:vc + pair_w]], axis=0)
        else:
            k_h = act_ref[r0 - BLK:r0 + BLK, kc:kc + pair_w]
            v_h = act_ref[r0 - BLK:r0 + BLK, vc:vc + pair_w]
        qs = []
        for p in range(GQA_GROUP // 2):
            c0 = A_AQ + (kvh * 2 + p) * pair_w
            qp = act_ref[rows, c0:c0 + pair_w]
            qs.append(jnp.where(lo, qp, zero_bf))
            qs.append(jnp.where(lo, zero_bf, qp))
        s_all = _dot_nt(jnp.concatenate(qs, axis=0), k_h)
        yield
        ps, invs = [], []
        for g in range(GQA_GROUP):
            sink = sink_ref[kvh * GQA_GROUP + g]
            s = jnp.where(valid, s_all[g * BLK:(g + 1) * BLK, :], NEG_INF)
            m = jnp.maximum(jnp.max(s, axis=-1, keepdims=True), sink)
            e = jnp.exp(s - m)
            den = jnp.sum(e, axis=-1, keepdims=True) + jnp.exp(sink - m)
            ps.append(e.astype(BF16))
            invs.append(1.0 / den)
        o_all = _dot(jnp.concatenate(ps, axis=0), v_h)
        yield
        assert writable(j), "mixing output traced before the previous tile's rows were read"
        for p in range(GQA_GROUP // 2):
            c0 = (kvh * 2 + p) * pair_w
            oe = o_all[(2 * p) * BLK:(2 * p + 1) * BLK, :] * invs[2 * p]
            oo = o_all[(2 * p + 1) * BLK:(2 * p + 2) * BLK, :] * invs[2 * p + 1]
            o_ref[rows, c0:c0 + pair_w] = jnp.where(lo, oe, oo).astype(BF16)

    def retention(j, heads):
        rows = slice(j * BLK, (j + 1) * BLK)
        part = []
        for hd in heads:
            c = hd * RET_HEAD_DIM
            q = act_ref[rows, A_RQ + c:A_RQ + c + RET_HEAD_DIM]
            k = act_ref[rows, A_RK + c:A_RK + c + RET_HEAD_DIM]
            v = act_ref[rows, A_RV + c:A_RV + c + RET_HEAD_DIM]
            vz = (v.astype(F32) * zeta_ref[hd]).astype(BF16)
            part.append((v, _dot_nt(q, k), _dot(q, state_ref[hd].astype(BF16)), _dot_tn(k, vz)))
        yield
        intra = []
        for hd, (v, sc, _, kv) in zip(heads, part, strict=True):
            state_ref[hd] = state_ref[hd] * decay_ref[hd] + kv
            intra.append(_dot((sc * dmat_ref[hd]).astype(BF16), v))
        yield
        assert writable(j), "mixing output traced before the previous tile's rows were read"
        for hd, (_, _, cross, _), y_intra in zip(heads, part, intra, strict=True):
            c = hd * RET_HEAD_DIM
            y = y_intra + cross * xi_ref[hd]
            mu = jnp.mean(y, axis=-1, keepdims=True)
            d = y - mu
            var = jnp.mean(d * d, axis=-1, keepdims=True)
            yn = d * lax.rsqrt(var + GN_EPS) * gnw_ref[:, c:c + RET_HEAD_DIM]
            sg = act_ref[rows, A_SG + c:A_SG + c + RET_HEAD_DIM]
            o_ref[rows, ATTN_WIDTH + c:ATTN_WIDTH + c + RET_HEAD_DIM] = (sg.astype(F32) * yn).astype(BF16)

    chains = []
    for j in range(BLKS_PER_TILE):
        chains += [attention(j, 0), attention(j, 1), retention(j, (0, 1)), retention(j, (2, 3))]
    active = []
    for chain in chains + [None] * (MIX_STAGES - 1):
        if chain is not None:
            active.append(chain)
        for g in list(active):
            try:
                next(g)
            except StopIteration:
                active.remove(g)
        yield


def _post_units(final, rows, consumed, ar_ref, h_ref, wo_ref, n_ref, wg_ref, wu_ref, wd_ref, fn_ref, o_ref,
                hmid_ref):
    h2 = h_ref[rows, :] + _dot(ar_ref[rows, :], wo_ref[...])
    consumed.add(rows.start // (rows.stop - rows.start))
    o_ref[rows, :] = h2
    xn = _rms(h2, n_ref[...]).astype(BF16)
    yield
    for c in range(D_FF // FF_CHUNK):
        sl = slice(c * FF_CHUNK, (c + 1) * FF_CHUNK)
        g = _dot(xn, wg_ref[:, sl])
        u = _dot(xn, wu_ref[:, sl])
        hmid_ref[rows, sl] = (jax.nn.silu(g) * u).astype(BF16)
        yield
    for c in range(D_MODEL // OUT_CHUNK):
        sl = slice(c * OUT_CHUNK, (c + 1) * OUT_CHUNK)
        o_ref[rows, sl] = o_ref[rows, sl] + 0.5 * _dot(hmid_ref[rows, :], wd_ref[:, sl])
        yield
    if final:
        o_ref[rows, :] = _rms(o_ref[rows, :], fn_ref[...])


def _interleave(*streams, early=()):
    total = max(n for _, n in streams)
    done = [0] * len(streams)
    for k in range(1, total + 1):
        for idx, (gen, n) in enumerate(streams):
            span = total - EARLY_ROUNDS if idx in early else total
            while done[idx] < -(-min(k, span) * n // span):
                next(gen)
                done[idx] += 1
    for gen, _ in streams:
        for _ in gen:
            raise AssertionError("stream yielded more often than declared")


def _mixpost_kernel(final, sink_ref, decay_ref, act_ref, kvp_ref, dmat_ref, zeta_ref, xi_ref, gnw_ref,
                    h_ref, wo_ref, n_ref, wg_ref, wu_ref, wd_ref, fn_ref, *rest):
    n_cast = 0 if final else 4
    cast_src, o_ref, cast_dst = rest[:n_cast], rest[n_cast], rest[n_cast + 1:2 * n_cast + 1]
    state_ref, ar_ref, hmid_ref = rest[2 * n_cast + 1:]
    t = pl.program_id(0)
    seq_start = (jnp.minimum(t, N_TILES - 1) % TILES_PER_SEQ) == 0

    @pl.when(seq_start)
    def _():
        state_ref[...] = jnp.zeros_like(state_ref)

    half = TM // 2

    def post_streams(consumed):
        args = (consumed, ar_ref, h_ref, wo_ref, n_ref, wg_ref, wu_ref, wd_ref, fn_ref, o_ref, hmid_ref)
        return ((_post_units(final, slice(0, half), *args), N_POST_UNITS),
                (_delayed(_post_units(final, slice(half, TM), *args), POST_SKEW), N_POST_UNITS + POST_SKEW))

    def mix_stream(consumed):
        return (_delayed(_mix_units(seq_start, consumed, sink_ref, decay_ref, act_ref, kvp_ref, dmat_ref, zeta_ref,
                                    xi_ref, gnw_ref, ar_ref, state_ref), MIX_DELAY), N_MIX_SLOTS + MIX_DELAY)

    @pl.when(t == 0)
    def _():
        _cast_blocks(cast_src, cast_dst)
        _interleave(mix_stream({0, 1}))

    @pl.when((t > 0) & (t < N_TILES))
    def _():
        _cast_blocks(cast_src, cast_dst)
        consumed = set()
        _interleave(*post_streams(consumed), mix_stream(consumed), early=(2,))

    @pl.when(t == N_TILES)
    def _():
        _cast_blocks(cast_src, cast_dst)
        _interleave(*post_streams(set()))


def _resident(shape, layer):
    nd = len(shape)
    return pl.BlockSpec((None,) + shape, lambda *_: (layer,) + (0,) * nd, pipeline_mode=pl.Buffered(1))


def _const(shape):
    nd = len(shape)
    return pl.BlockSpec(shape, lambda *_: (0,) * nd, pipeline_mode=pl.Buffered(1))


def _cast_specs(layer, shapes):
    ins, outs, shapes_out = [], [], []
    for rows, cols in shapes:
        blk = CAST_ROWS[rows]
        last = rows // blk - 1
        ins.append(pl.BlockSpec((None, blk, cols), lambda i, last=last: (layer, jnp.minimum(i, last), 0)))
        outs.append(pl.BlockSpec((blk, cols), lambda i, last=last: (jnp.minimum(i, last), 0)))
        shapes_out.append(jax.ShapeDtypeStruct((rows, cols), BF16))
    return ins, outs, shapes_out


def _pre_call(layer, x, n1, wg, wu, wd, n2, win, cos_t, sin_t, next_f32):
    tok = lambda w: pl.BlockSpec((TM, w), lambda i: (i, 0))
    cast_in, cast_out, cast_shapes = _cast_specs(
        layer, ((D_MODEL, D_MODEL), (D_MODEL, D_FF), (D_MODEL, D_FF), (D_FF, D_MODEL)))
    return pl.pallas_call(
        _pre_kernel,
        grid=(N_TILES,),
        in_specs=[
            tok(D_MODEL),
            _resident((1, D_MODEL), layer),
            _const((D_MODEL, D_FF)),
            _const((D_MODEL, D_FF)),
            _const((D_FF, D_MODEL)),
            _resident((1, D_MODEL), layer),
            _const((D_MODEL, D_IN)),
            _const((SEQ, RET_HEAD_DIM)),
            _const((SEQ, RET_HEAD_DIM)),
        ] + cast_in,
        out_specs=[tok(D_MODEL), tok(D_ACT)] + cast_out,
        out_shape=[jax.ShapeDtypeStruct((N_TOK, D_MODEL), F32),
                   jax.ShapeDtypeStruct((N_TOK, D_ACT), BF16)] + cast_shapes,
        scratch_shapes=[pltpu.VMEM((TM, D_FF), BF16)],
        compiler_params=pltpu.CompilerParams(dimension_semantics=("arbitrary",), vmem_limit_bytes=VMEM_LIMIT),
        name=f"pre{layer}",
    )(x, n1, wg, wu, wd, n2, win, cos_t, sin_t, *next_f32)


def _mixpost_call(layer, final, sinks, decay, act, dmat, zeta_b, xi_b, gnw, h, wo, n, wg, wu, wd, fn, next_f32):
    def mix_tile(i):
        return jnp.minimum(i, N_TILES - 1)

    if final:
        cast_in, cast_out, cast_shapes = [], [], []
    else:
        cast_in, cast_out, cast_shapes = _cast_specs(
            layer + 1, ((D_MODEL, D_FF), (D_MODEL, D_FF), (D_FF, D_MODEL), (D_MODEL, D_IN)))

    prev_kv = pl.BlockSpec((BLK, A_RQ - A_KD),
                           lambda i: (jnp.maximum(mix_tile(i) * BLKS_PER_TILE - 1, 0), A_KD // (A_RQ - A_KD)))
    lag = pl.BlockSpec((TM, D_MODEL), lambda i: (jnp.maximum(i - 1, 0), 0))
    smem = pl.BlockSpec(memory_space=pltpu.SMEM)
    return pl.pallas_call(
        functools.partial(_mixpost_kernel, final),
        grid=(N_TILES + 1,),
        in_specs=[smem, smem,
                  pl.BlockSpec((TM, D_ACT), lambda i: (mix_tile(i), 0)),
                  prev_kv,
                  _const((N_RET_HEADS, BLK, BLK)), _const((N_RET_HEADS, BLK, BLK)), _const((N_RET_HEADS, BLK, BLK)),
                  _resident((1, RET_WIDTH), layer),
                  lag,
                  _const((D_MODEL, D_MODEL)),
                  _resident((1, D_MODEL), layer),
                  _const((D_MODEL, D_FF)),
                  _const((D_MODEL, D_FF)),
                  _const((D_FF, D_MODEL)),
                  _const((1, D_MODEL))] + cast_in,
        out_specs=[lag] + cast_out,
        out_shape=[jax.ShapeDtypeStruct((N_TOK, D_MODEL), F32)] + cast_shapes,
        scratch_shapes=[pltpu.VMEM((N_RET_HEADS, RET_HEAD_DIM, RET_HEAD_DIM), F32),
                        pltpu.VMEM((TM, D_MODEL), BF16),
                        pltpu.VMEM((TM, D_FF), BF16)],
        compiler_params=pltpu.CompilerParams(dimension_semantics=("arbitrary",), vmem_limit_bytes=VMEM_LIMIT),
        name=f"mixpost{layer}",
    )(sinks, decay, act, act, dmat, zeta_b, xi_b, gnw, h, wo, n, wg, wu, wd, fn, *next_f32)


def _tables():
    pos = jnp.arange(SEQ, dtype=F32)
    inv_freq = ROPE_BASE ** (-jnp.arange(0, RET_HEAD_DIM, 2, dtype=F32) / RET_HEAD_DIM)
    ang = pos[:, None] * inv_freq[None, :]
    cos, sin = jnp.cos(ang), jnp.sin(ang)
    cos_t = jnp.concatenate([cos, cos], axis=-1)
    sin_t = jnp.concatenate([-sin, sin], axis=-1)
    log_gamma = jnp.log(1.0 - 2.0 ** (-5.0 - jnp.arange(N_RET_HEADS, dtype=F32)))
    idx = jnp.arange(BLK, dtype=F32)
    dif = idx[:, None] - idx[None, :]
    dmat = jnp.where(dif[None] >= 0, jnp.exp(jnp.maximum(dif, 0.0)[None] * log_gamma[:, None, None]), 0.0)
    zeta = jnp.exp((BLK - 1.0 - idx)[None, :] * log_gamma[:, None])
    xi = jnp.exp((idx + 1.0)[None, :] * log_gamma[:, None])
    decay = jnp.exp(BLK * log_gamma)
    bshape = (N_RET_HEADS, BLK, RET_HEAD_DIM)
    zeta_b = jnp.broadcast_to(zeta[:, :, None], bshape)
    xi_b = jnp.broadcast_to(xi[:, :, None], bshape)
    return cos_t, sin_t, dmat, zeta_b, xi_b, decay


def kernel(x, ffn1_norm, ffn1_w_gate, ffn1_w_up, ffn1_w_down, mix_norm, w_in, attn_sinks, ret_gn_w, w_out,
           ffn2_norm, ffn2_w_gate, ffn2_w_up, ffn2_w_down, final_norm):
    cos_t, sin_t, dmat, zeta_b, xi_b, decay = _tables()
    pre_w = [w[0].astype(BF16) for w in (ffn1_w_gate, ffn1_w_up, ffn1_w_down, w_in)]
    pre_f32 = (ffn1_w_gate, ffn1_w_up, ffn1_w_down, w_in)
    post_f32 = (w_out, ffn2_w_gate, ffn2_w_up, ffn2_w_down)
    n1 = ffn1_norm.reshape(DEPTH, 1, D_MODEL)
    nm = mix_norm.reshape(DEPTH, 1, D_MODEL)
    n2 = ffn2_norm.reshape(DEPTH, 1, D_MODEL)
    gnw = ret_gn_w.reshape(DEPTH, 1, RET_WIDTH)
    fn = final_norm.reshape(1, D_MODEL)

    h = x.reshape(N_TOK, D_MODEL)
    for layer in range(DEPTH):
        final = layer == DEPTH - 1
        wg1, wu1, wd1, win = pre_w
        h, act, wo, wg2, wu2, wd2 = _pre_call(layer, h, n1, wg1, wu1, wd1, nm, win, cos_t, sin_t, post_f32)
        h, *pre_w = _mixpost_call(layer, final, attn_sinks[layer], decay, act, dmat, zeta_b, xi_b, gnw,
                                  h, wo, n2, wg2, wu2, wd2, fn, () if final else pre_f32)
    return h.reshape(BATCH, SEQ, D_MODEL)
```

```python
import functools

import jax
import jax.numpy as jnp
from jax import lax
from jax.experimental import pallas as pl
from jax.experimental.pallas import tpu as pltpu

D_MODEL = 1024
BATCH = 8
SEQ = 2048
DEPTH = 4
ATTN_WIDTH = 512
RET_WIDTH = 512
HEAD_DIM = 64
N_KV_HEADS = 2
GQA_GROUP = 4
KV_WIDTH = 128
WINDOW = 128
BLK = 128
N_RET_HEADS = 4
RET_HEAD_DIM = 128
ROPE_BASE = 10000.0
D_FF = 2816
D_IN = 2816
NORM_EPS = 1e-6
GN_EPS = 1e-5
NEG_INF = -1e30
N_TOK = BATCH * SEQ

W_AQ, W_KV, W_RQ, W_RK, W_RV, W_RG = 0, 512, 768, 1280, 1792, 2304
A_AQ, A_KD, A_VD, A_RQ, A_RK, A_RV, A_SG, D_ACT = 0, 512, 768, 1024, 1536, 2048, 2560, 3072

TM = 512
N_TILES = N_TOK // TM
TILES_PER_SEQ = SEQ // TM
BLKS_PER_TILE = TM // BLK
FF_CHUNK = 256
OUT_CHUNK = 512
N_PRE_UNITS = 1 + D_FF // FF_CHUNK + D_MODEL // OUT_CHUNK + 6
PRE_SKEW = 3
MIX_STAGES = 3
N_MIX_SLOTS = 4 * BLKS_PER_TILE + MIX_STAGES - 1
N_POST_UNITS = 1 + D_FF // FF_CHUNK + D_MODEL // OUT_CHUNK
POST_SKEW = 2
MIX_DELAY = 1
EARLY_ROUNDS = 3
CAST_ROWS = {D_MODEL: D_MODEL // N_TILES, D_FF: D_FF // (N_TILES // 2)}
VMEM_LIMIT = 56 * 1024 * 1024

F32 = jnp.float32
BF16 = jnp.bfloat16


def _dot(a, b):
    return jnp.dot(a, b, preferred_element_type=F32)


def _dot_nt(a, b):
    return lax.dot_general(a, b, (((1,), (1,)), ((), ())), preferred_element_type=F32)


def _dot_tn(a, b):
    return lax.dot_general(a, b, (((0,), (0,)), ((), ())), preferred_element_type=F32)


def _rms(x, w):
    y = x * lax.rsqrt(jnp.mean(x * x, axis=-1, keepdims=True) + NORM_EPS)
    return y * w


def _swap_halves(x):
    return pltpu.roll(x, x.shape[-1] // 2, 1)


def _cast_blocks(src_refs, dst_refs):
    for src, dst in zip(src_refs, dst_refs, strict=True):
        dst[...] = src[...].astype(BF16)


def _pre_units(rows, arrive, x_ref, n1_ref, wg_ref, wu_ref, wd_ref, n2_ref, win_ref, cos_ref, sin_ref,
               h_ref, act_ref, hmid_ref):
    def wait_for(phase):
        for copy in arrive.get(phase, ()):
            copy.wait()

    nrow = rows.stop - rows.start
    wait_for("ffn_in")
    x = x_ref[rows, :]
    xw = (x * n1_ref[...]).astype(BF16)
    r = jnp.broadcast_to(lax.rsqrt(jnp.mean(x * x, axis=-1, keepdims=True) + NORM_EPS), (nrow, FF_CHUNK))
    yield
    for c in range(D_FF // FF_CHUNK):
        sl = slice(c * FF_CHUNK, (c + 1) * FF_CHUNK)
        g = _dot(xw, wg_ref[:, sl]) * r
        u = _dot(xw, wu_ref[:, sl]) * r
        hmid_ref[rows, sl] = (jax.nn.silu(g) * u).astype(BF16)
        yield
    wait_for("ffn_out")
    for c in range(D_MODEL // OUT_CHUNK):
        sl = slice(c * OUT_CHUNK, (c + 1) * OUT_CHUNK)
        h_ref[rows, sl] = x_ref[rows, sl] + 0.5 * _dot(hmid_ref[rows, :], wd_ref[:, sl])
        yield
    wait_for("proj")
    u = _rms(h_ref[rows, :], n2_ref[...]).astype(BF16)

    def proj(lo, hi):
        return _dot(u, win_ref[:, lo:hi])

    act_ref[rows, A_AQ:A_KD] = (proj(W_AQ, W_KV) * (HEAD_DIM ** -0.5)).astype(BF16)
    yield

    first = lax.broadcasted_iota(jnp.int32, (nrow // 2, KV_WIDTH), 1) < HEAD_DIM
    for part in range(2):
        prow = slice(rows.start + part * (nrow // 2), rows.start + (part + 1) * (nrow // 2))
        kv = _dot(u[part * (nrow // 2):(part + 1) * (nrow // 2), :], win_ref[:, W_KV:W_RQ])
        for src, dst in ((0, A_KD), (KV_WIDTH, A_VD)):
            t = kv[:, src:src + KV_WIDTH]
            ts = _swap_halves(t)
            act_ref[prow, dst:dst + KV_WIDTH] = jnp.where(first, t, ts).astype(BF16)
            act_ref[prow, dst + KV_WIDTH:dst + 2 * KV_WIDTH] = jnp.where(first, ts, t).astype(BF16)
    yield

    pos0 = pl.multiple_of((pl.program_id(0) % TILES_PER_SEQ) * TM + rows.start, nrow)
    cos = cos_ref[pl.ds(pos0, nrow), :]
    sin = sin_ref[pl.ds(pos0, nrow), :]
    for w0, a0, scale in ((W_RQ, A_RQ, 1.0), (W_RK, A_RK, RET_HEAD_DIM ** -0.5)):
        z = proj(w0, w0 + RET_WIDTH)
        for hd in range(N_RET_HEADS):
            zh = z[:, hd * RET_HEAD_DIM:(hd + 1) * RET_HEAD_DIM]
            zr = zh * cos + _swap_halves(zh) * sin
            if scale != 1.0:
                zr = zr * scale
            act_ref[rows, a0 + hd * RET_HEAD_DIM:a0 + (hd + 1) * RET_HEAD_DIM] = zr.astype(BF16)
        yield
    act_ref[rows, A_SG:D_ACT] = jax.nn.silu(proj(W_RG, D_IN)).astype(BF16)
    yield
    act_ref[rows, A_RV:A_SG] = proj(W_RV, W_RG).astype(BF16)
    yield


def _delayed(gen, n):
    for _ in range(n):
        yield
    yield from gen


def _pre_kernel(x_ref, n1_ref, wg_hbm, wu_hbm, wd_hbm, n2_ref, win_hbm, cos_ref, sin_ref, *rest):
    cast_src, (h_ref, act_ref), cast_dst = rest[:4], rest[4:6], rest[6:10]
    hmid_ref, wg_ref, wu_ref, wd_ref, win_ref, wsem = rest[10:]
    copies = [pltpu.make_async_copy(src, dst, wsem.at[i]) for i, (src, dst) in enumerate(
        ((wg_hbm, wg_ref), (wu_hbm, wu_ref), (wd_hbm, wd_ref), (win_hbm, win_ref)))]
    args = (x_ref, n1_ref, wg_ref, wu_ref, wd_ref, n2_ref, win_ref, cos_ref, sin_ref, h_ref, act_ref, hmid_ref)
    half = TM // 2

    def body(arrive):
        _cast_blocks(cast_src, cast_dst)
        _interleave((_pre_units(slice(0, half), arrive, *args), N_PRE_UNITS),
                    (_delayed(_pre_units(slice(half, TM), {}, *args), PRE_SKEW), N_PRE_UNITS + PRE_SKEW))

    @pl.when(pl.program_id(0) == 0)
    def _():
        for copy in copies:
            copy.start()
        body({"ffn_in": copies[:2], "ffn_out": copies[2:3], "proj": copies[3:]})

    @pl.when(pl.program_id(0) > 0)
    def _():
        body({})


def _mix_units(seq_start, consumed, sink_ref, decay_ref, act_ref, kvp_ref, dmat_ref, zeta_ref, xi_ref, gnw_ref,
               o_ref, state_ref):
    def writable(j):
        return (j * BLK) // (TM // 2) in consumed

    row = lax.broadcasted_iota(jnp.int32, (BLK, 2 * BLK), 0)
    col = lax.broadcasted_iota(jnp.int32, (BLK, 2 * BLK), 1)
    band = (col > row) & (col <= row + WINDOW)
    band0 = band & ((col >= BLK) | jnp.logical_not(seq_start))
    lo = lax.broadcasted_iota(jnp.int32, (BLK, 2 * HEAD_DIM), 1) < HEAD_DIM
    zero_bf = jnp.zeros((BLK, 2 * HEAD_DIM), BF16)
    pair_w = 2 * HEAD_DIM

    def attention(j, kvh):
        r0 = j * BLK
        rows = slice(r0, r0 + BLK)
        valid = band0 if j == 0 else band
        kc = A_KD + kvh * pair_w
        vc = A_VD + kvh * pair_w
        if j == 0:
            k_h = jnp.concatenate([kvp_ref[:, kc - A_KD:kc - A_KD + pair_w], act_ref[0:BLK, kc:kc + pair_w]], axis=0)
            v_h = jnp.concatenate([kvp_ref[:, vc - A_KD:vc - A_KD + pair_w], act_ref[0:BLK, vc:vc + pair_w]], axis=0)
        else:
            k_h = act_ref[r0 - BLK:r0 + BLK, kc:kc + pair_w]
            v_h = act_ref[r0 - BLK:r0 + BLK, vc:vc + pair_w]
        qs = []
        for p in range(GQA_GROUP // 2):
            c0 = A_AQ + (kvh * 2 + p) * pair_w
            qp = act_ref[rows, c0:c0 + pair_w]
            qs.append(jnp.where(lo, qp, zero_bf))
            qs.append(jnp.where(lo, zero_bf, qp))
        s_all = _dot_nt(jnp.concatenate(qs, axis=0), k_h)
        yield
        ps, invs = [], []
        for g in range(GQA_GROUP):
            sink = sink_ref[kvh * GQA_GROUP + g]
            s = jnp.where(valid, s_all[g * BLK:(g + 1) * BLK, :], NEG_INF)
            m = jnp.maximum(jnp.max(s, axis=-1, keepdims=True), sink)
            e = jnp.exp(s - m)
            den = jnp.sum(e, axis=-1, keepdims=True) + jnp.exp(sink - m)
            ps.append(e.astype(BF16))
            invs.append(1.0 / den)
        o_all = _dot(jnp.concatenate(ps, axis=0), v_h)
        yield
        assert writable(j), "mixing output traced before the previous tile's rows were read"
        for p in range(GQA_GROUP // 2):
            c0 = (kvh * 2 + p) * pair_w
            oe = o_all[(2 * p) * BLK:(2 * p + 1) * BLK, :] * invs[2 * p]
            oo = o_all[(2 * p + 1) * BLK:(2 * p + 2) * BLK, :] * invs[2 * p + 1]
            o_ref[rows, c0:c0 + pair_w] = jnp.where(lo, oe, oo).astype(BF16)

    def retention(j, heads):
        rows = slice(j * BLK, (j + 1) * BLK)
        part = []
        for hd in heads:
            c = hd * RET_HEAD_DIM
            q = act_ref[rows, A_RQ + c:A_RQ + c + RET_HEAD_DIM]
            k = act_ref[rows, A_RK + c:A_RK + c + RET_HEAD_DIM]
            v = act_ref[rows, A_RV + c:A_RV + c + RET_HEAD_DIM]
            vz = (v.astype(F32) * zeta_ref[hd]).astype(BF16)
            part.append((v, _dot_nt(q, k), _dot(q, state_ref[hd].astype(BF16)), _dot_tn(k, vz)))
        yield
        intra = []
        for hd, (v, sc, _, kv) in zip(heads, part, strict=True):
            state_ref[hd] = state_ref[hd] * decay_ref[hd] + kv
            intra.append(_dot((sc * dmat_ref[hd]).astype(BF16), v))
        yield
        assert writable(j), "mixing output traced before the previous tile's rows were read"
        for hd, (_, _, cross, _), y_intra in zip(heads, part, intra, strict=True):
            c = hd * RET_HEAD_DIM
            y = y_intra + cross * xi_ref[hd]
            mu = jnp.mean(y, axis=-1, keepdims=True)
            d = y - mu
            var = jnp.mean(d * d, axis=-1, keepdims=True)
            yn = d * lax.rsqrt(var + GN_EPS) * gnw_ref[:, c:c + RET_HEAD_DIM]
            sg = act_ref[rows, A_SG + c:A_SG + c + RET_HEAD_DIM]
            o_ref[rows, ATTN_WIDTH + c:ATTN_WIDTH + c + RET_HEAD_DIM] = (sg.astype(F32) * yn).astype(BF16)

    chains = []
    for j in range(BLKS_PER_TILE):
        chains += [attention(j, 0), attention(j, 1), retention(j, (0, 1)), retention(j, (2, 3))]
    active = []
    for chain in chains + [None] * (MIX_STAGES - 1):
        if chain is not None:
            active.append(chain)
        for g in list(active):
            try:
                next(g)
            except StopIteration:
                active.remove(g)
        yield


def _post_units(final, rows, consumed, ar_ref, h_ref, wo_ref, n_ref, wg_ref, wu_ref, wd_ref, fn_ref, o_ref,
                hmid_ref):
    h2 = h_ref[rows, :] + _dot(ar_ref[rows, :], wo_ref[...])
    consumed.add(rows.start // (rows.stop - rows.start))
    o_ref[rows, :] = h2
    xn = _rms(h2, n_ref[...]).astype(BF16)
    yield
    for c in range(D_FF // FF_CHUNK):
        sl = slice(c * FF_CHUNK, (c + 1) * FF_CHUNK)
        g = _dot(xn, wg_ref[:, sl])
        u = _dot(xn, wu_ref[:, sl])
        hmid_ref[rows, sl] = (jax.nn.silu(g) * u).astype(BF16)
        yield
    for c in range(D_MODEL // OUT_CHUNK):
        sl = slice(c * OUT_CHUNK, (c + 1) * OUT_CHUNK)
        o_ref[rows, sl] = o_ref[rows, sl] + 0.5 * _dot(hmid_ref[rows, :], wd_ref[:, sl])
        yield
    if final:
        o_ref[rows, :] = _rms(o_ref[rows, :], fn_ref[...])


def _interleave(*streams, early=()):
    total = max(n for _, n in streams)
    done = [0] * len(streams)
    for k in range(1, total + 1):
        for idx, (gen, n) in enumerate(streams):
            span = total - EARLY_ROUNDS if idx in early else total
            while done[idx] < -(-min(k, span) * n // span):
                next(gen)
                done[idx] += 1
    for gen, _ in streams:
        for _ in gen:
            raise AssertionError("stream yielded more often than declared")


def _mixpost_kernel(final, sink_ref, decay_ref, act_ref, kvp_ref, dmat_ref, zeta_ref, xi_ref, gnw_ref,
                    h_ref, wo_hbm, n_ref, wg_hbm, wu_hbm, wd_hbm, fn_ref, *rest):
    n_cast = 0 if final else 4
    cast_src, o_ref, cast_dst = rest[:n_cast], rest[n_cast], rest[n_cast + 1:2 * n_cast + 1]
    state_ref, ar_ref, hmid_ref, wo_ref, wg_ref, wu_ref, wd_ref, wsem = rest[2 * n_cast + 1:]
    weight_copies = [pltpu.make_async_copy(src, dst, wsem.at[i]) for i, (src, dst) in enumerate(
        ((wo_hbm, wo_ref), (wg_hbm, wg_ref), (wu_hbm, wu_ref), (wd_hbm, wd_ref)))]
    t = pl.program_id(0)
    seq_start = (jnp.minimum(t, N_TILES - 1) % TILES_PER_SEQ) == 0

    @pl.when(seq_start)
    def _():
        state_ref[...] = jnp.zeros_like(state_ref)

    half = TM // 2

    def post_streams(consumed):
        args = (consumed, ar_ref, h_ref, wo_ref, n_ref, wg_ref, wu_ref, wd_ref, fn_ref, o_ref, hmid_ref)
        return ((_post_units(final, slice(0, half), *args), N_POST_UNITS),
                (_delayed(_post_units(final, slice(half, TM), *args), POST_SKEW), N_POST_UNITS + POST_SKEW))

    def mix_stream(consumed):
        return (_delayed(_mix_units(seq_start, consumed, sink_ref, decay_ref, act_ref, kvp_ref, dmat_ref, zeta_ref,
                                    xi_ref, gnw_ref, ar_ref, state_ref), MIX_DELAY), N_MIX_SLOTS + MIX_DELAY)

    @pl.when(t == 0)
    def _():
        for copy in weight_copies:
            copy.start()
        _cast_blocks(cast_src, cast_dst)
        _interleave(mix_stream({0, 1}))
        for copy in weight_copies:
            copy.wait()

    @pl.when((t > 0) & (t < N_TILES))
    def _():
        _cast_blocks(cast_src, cast_dst)
        consumed = set()
        _interleave(*post_streams(consumed), mix_stream(consumed), early=(2,))

    @pl.when(t == N_TILES)
    def _():
        _cast_blocks(cast_src, cast_dst)
        _interleave(*post_streams(set()))


HBM = pl.BlockSpec(memory_space=pl.ANY)


def _resident(shape, layer):
    nd = len(shape)
    return pl.BlockSpec((None,) + shape, lambda *_: (layer,) + (0,) * nd, pipeline_mode=pl.Buffered(1))


def _const(shape):
    nd = len(shape)
    return pl.BlockSpec(shape, lambda *_: (0,) * nd, pipeline_mode=pl.Buffered(1))


def _cast_specs(layer, shapes):
    ins, outs, shapes_out = [], [], []
    for rows, cols in shapes:
        blk = CAST_ROWS[rows]
        last = rows // blk - 1
        ins.append(pl.BlockSpec((None, blk, cols), lambda i, last=last: (layer, jnp.minimum(i, last), 0)))
        outs.append(pl.BlockSpec((blk, cols), lambda i, last=last: (jnp.minimum(i, last), 0)))
        shapes_out.append(jax.ShapeDtypeStruct((rows, cols), BF16))
    return ins, outs, shapes_out


def _pre_call(layer, x, n1, wg, wu, wd, n2, win, cos_t, sin_t, next_f32):
    tok = lambda w: pl.BlockSpec((TM, w), lambda i: (i, 0))
    cast_in, cast_out, cast_shapes = _cast_specs(
        layer, ((D_MODEL, D_MODEL), (D_MODEL, D_FF), (D_MODEL, D_FF), (D_FF, D_MODEL)))
    return pl.pallas_call(
        _pre_kernel,
        grid=(N_TILES,),
        in_specs=[
            tok(D_MODEL),
            _resident((1, D_MODEL), layer),
            HBM, HBM, HBM,
            _resident((1, D_MODEL), layer),
            HBM,
            _const((SEQ, RET_HEAD_DIM)),
            _const((SEQ, RET_HEAD_DIM)),
        ] + cast_in,
        out_specs=[tok(D_MODEL), tok(D_ACT)] + cast_out,
        out_shape=[jax.ShapeDtypeStruct((N_TOK, D_MODEL), F32),
                   jax.ShapeDtypeStruct((N_TOK, D_ACT), BF16)] + cast_shapes,
        scratch_shapes=[pltpu.VMEM((TM, D_FF), BF16),
                        pltpu.VMEM((D_MODEL, D_FF), BF16),
                        pltpu.VMEM((D_MODEL, D_FF), BF16),
                        pltpu.VMEM((D_FF, D_MODEL), BF16),
                        pltpu.VMEM((D_MODEL, D_IN), BF16),
                        pltpu.SemaphoreType.DMA((4,))],
        compiler_params=pltpu.CompilerParams(dimension_semantics=("arbitrary",), vmem_limit_bytes=VMEM_LIMIT),
        name=f"pre{layer}",
    )(x, n1, wg, wu, wd, n2, win, cos_t, sin_t, *next_f32)


def _mixpost_call(layer, final, sinks, decay, act, dmat, zeta_b, xi_b, gnw, h, wo, n, wg, wu, wd, fn, next_f32):
    def mix_tile(i):
        return jnp.minimum(i, N_TILES - 1)

    if final:
        cast_in, cast_out, cast_shapes = [], [], []
    else:
        cast_in, cast_out, cast_shapes = _cast_specs(
            layer + 1, ((D_MODEL, D_FF), (D_MODEL, D_FF), (D_FF, D_MODEL), (D_MODEL, D_IN)))

    prev_kv = pl.BlockSpec((BLK, A_RQ - A_KD),
                           lambda i: (jnp.maximum(mix_tile(i) * BLKS_PER_TILE - 1, 0), A_KD // (A_RQ - A_KD)))
    lag = pl.BlockSpec((TM, D_MODEL), lambda i: (jnp.maximum(i - 1, 0), 0))
    smem = pl.BlockSpec(memory_space=pltpu.SMEM)
    return pl.pallas_call(
        functools.partial(_mixpost_kernel, final),
        grid=(N_TILES + 1,),
        in_specs=[smem, smem,
                  pl.BlockSpec((TM, D_ACT), lambda i: (mix_tile(i), 0)),
                  prev_kv,
                  _const((N_RET_HEADS, BLK, BLK)), _const((N_RET_HEADS, BLK, BLK)), _const((N_RET_HEADS, BLK, BLK)),
                  _resident((1, RET_WIDTH), layer),
                  lag,
                  HBM,
                  _resident((1, D_MODEL), layer),
                  HBM, HBM, HBM,
                  _const((1, D_MODEL))] + cast_in,
        out_specs=[lag] + cast_out,
        out_shape=[jax.ShapeDtypeStruct((N_TOK, D_MODEL), F32)] + cast_shapes,
        scratch_shapes=[pltpu.VMEM((N_RET_HEADS, RET_HEAD_DIM, RET_HEAD_DIM), F32),
                        pltpu.VMEM((TM, D_MODEL), BF16),
                        pltpu.VMEM((TM, D_FF), BF16),
                        pltpu.VMEM((D_MODEL, D_MODEL), BF16),
                        pltpu.VMEM((D_MODEL, D_FF), BF16),
                        pltpu.VMEM((D_MODEL, D_FF), BF16),
                        pltpu.VMEM((D_FF, D_MODEL), BF16),
                        pltpu.SemaphoreType.DMA((4,))],
        compiler_params=pltpu.CompilerParams(dimension_semantics=("arbitrary",), vmem_limit_bytes=VMEM_LIMIT),
        name=f"mixpost{layer}",
    )(sinks, decay, act, act, dmat, zeta_b, xi_b, gnw, h, wo, n, wg, wu, wd, fn, *next_f32)


def _tables():
    pos = jnp.arange(SEQ, dtype=F32)
    inv_freq = ROPE_BASE ** (-jnp.arange(0, RET_HEAD_DIM, 2, dtype=F32) / RET_HEAD_DIM)
    ang = pos[:, None] * inv_freq[None, :]
    cos, sin = jnp.cos(ang), jnp.sin(ang)
    cos_t = jnp.concatenate([cos, cos], axis=-1)
    sin_t = jnp.concatenate([-sin, sin], axis=-1)
    log_gamma = jnp.log(1.0 - 2.0 ** (-5.0 - jnp.arange(N_RET_HEADS, dtype=F32)))
    idx = jnp.arange(BLK, dtype=F32)
    dif = idx[:, None] - idx[None, :]
    dmat = jnp.where(dif[None] >= 0, jnp.exp(jnp.maximum(dif, 0.0)[None] * log_gamma[:, None, None]), 0.0)
    zeta = jnp.exp((BLK - 1.0 - idx)[None, :] * log_gamma[:, None])
    xi = jnp.exp((idx + 1.0)[None, :] * log_gamma[:, None])
    decay = jnp.exp(BLK * log_gamma)
    bshape = (N_RET_HEADS, BLK, RET_HEAD_DIM)
    zeta_b = jnp.broadcast_to(zeta[:, :, None], bshape)
    xi_b = jnp.broadcast_to(xi[:, :, None], bshape)
    return cos_t, sin_t, dmat, zeta_b, xi_b, decay


def kernel(x, ffn1_norm, ffn1_w_gate, ffn1_w_up, ffn1_w_down, mix_norm, w_in, attn_sinks, ret_gn_w, w_out,
           ffn2_norm, ffn2_w_gate, ffn2_w_up, ffn2_w_down, final_norm):
    cos_t, sin_t, dmat, zeta_b, xi_b, decay = _tables()
    pre_w = [w[0].astype(BF16) for w in (ffn1_w_gate, ffn1_w_up, ffn1_w_down, w_in)]
    pre_f32 = (ffn1_w_gate, ffn1_w_up, ffn1_w_down, w_in)
    post_f32 = (w_out, ffn2_w_gate, ffn2_w_up, ffn2_w_down)
    n1 = ffn1_norm.reshape(DEPTH, 1, D_MODEL)
    nm = mix_norm.reshape(DEPTH, 1, D_MODEL)
    n2 = ffn2_norm.reshape(DEPTH, 1, D_MODEL)
    gnw = ret_gn_w.reshape(DEPTH, 1, RET_WIDTH)
    fn = final_norm.reshape(1, D_MODEL)

    h = x.reshape(N_TOK, D_MODEL)
    for layer in range(DEPTH):
        final = layer == DEPTH - 1
        wg1, wu1, wd1, win = pre_w
        h, act, wo, wg2, wu2, wd2 = _pre_call(layer, h, n1, wg1, wu1, wd1, nm, win, cos_t, sin_t, post_f32)
        h, *pre_w = _mixpost_call(layer, final, attn_sinks[layer], decay, act, dmat, zeta_b, xi_b, gnw,
                                  h, wo, n2, wg2, wu2, wd2, fn, () if final else pre_f32)
    return h.reshape(BATCH, SEQ, D_MODEL)
```

```python
import functools

import jax
import jax.numpy as jnp
from jax import lax
from jax.experimental import pallas as pl
from jax.experimental.pallas import tpu as pltpu

D_MODEL = 1024
BATCH = 8
SEQ = 2048
DEPTH = 4
ATTN_WIDTH = 512
RET_WIDTH = 512
HEAD_DIM = 64
N_KV_HEADS = 2
GQA_GROUP = 4
KV_WIDTH = 128
WINDOW = 128
BLK = 128
N_RET_HEADS = 4
RET_HEAD_DIM = 128
ROPE_BASE = 10000.0
D_FF = 2816
D_IN = 2816
NORM_EPS = 1e-6
GN_EPS = 1e-5
NEG_INF = -1e30
N_TOK = BATCH * SEQ

W_AQ, W_KV, W_RQ, W_RK, W_RV, W_RG = 0, 512, 768, 1280, 1792, 2304
A_AQ, A_KD, A_VD, A_RQ, A_RK, A_RV, A_SG, D_ACT = 0, 512, 768, 1024, 1536, 2048, 2560, 3072

TM = 512
N_TILES = N_TOK // TM
TILES_PER_SEQ = SEQ // TM
BLKS_PER_TILE = TM // BLK
FF_CHUNK = 256
OUT_CHUNK = 512
N_PRE_UNITS = 1 + D_FF // FF_CHUNK + D_MODEL // OUT_CHUNK + 6
PRE_SKEW = 3
MIX_STAGES = 3
N_MIX_SLOTS = 4 * BLKS_PER_TILE + MIX_STAGES - 1
N_POST_UNITS = 1 + D_FF // FF_CHUNK + D_MODEL // OUT_CHUNK
POST_SKEW = 2
MIX_DELAY = 1
EARLY_ROUNDS = 3
CAST_ROWS = {D_MODEL: D_MODEL // N_TILES, D_FF: D_FF // (N_TILES // 2)}
VMEM_LIMIT = 56 * 1024 * 1024
GAIN_ROWS = 8

F32 = jnp.float32
BF16 = jnp.bfloat16


def _dot(a, b):
    return jnp.dot(a, b, preferred_element_type=F32)


def _dot_nt(a, b):
    return lax.dot_general(a, b, (((1,), (1,)), ((), ())), preferred_element_type=F32)


def _dot_tn(a, b):
    return lax.dot_general(a, b, (((0,), (0,)), ((), ())), preferred_element_type=F32)


def _rms(x, w):
    y = x * lax.rsqrt(jnp.mean(x * x, axis=-1, keepdims=True) + NORM_EPS)
    return y * w


def _swap_halves(x):
    return pltpu.roll(x, x.shape[-1] // 2, 1)


def _cast_blocks(src_refs, dst_refs):
    for src, dst in zip(src_refs, dst_refs, strict=True):
        dst[...] = src[...].astype(BF16)


def _pre_units(rows, x_ref, n1_ref, wg_ref, wu_ref, wd_ref, n2_ref, win_ref, cos_ref, sin_ref,
               h_ref, act_ref, hmid_ref):
    nrow = rows.stop - rows.start
    x = x_ref[rows, :]
    xw = (x * n1_ref[0:1, :]).astype(BF16)
    r = jnp.broadcast_to(lax.rsqrt(jnp.mean(x * x, axis=-1, keepdims=True) + NORM_EPS), (nrow, FF_CHUNK))
    yield
    for c in range(D_FF // FF_CHUNK):
        sl = slice(c * FF_CHUNK, (c + 1) * FF_CHUNK)
        g = _dot(xw, wg_ref[:, sl]) * r
        u = _dot(xw, wu_ref[:, sl]) * r
        hmid_ref[rows, sl] = (jax.nn.silu(g) * u).astype(BF16)
        yield
    for c in range(D_MODEL // OUT_CHUNK):
        sl = slice(c * OUT_CHUNK, (c + 1) * OUT_CHUNK)
        h_ref[rows, sl] = x_ref[rows, sl] + 0.5 * _dot(hmid_ref[rows, :], wd_ref[:, sl])
        yield
    u = _rms(h_ref[rows, :], n2_ref[0:1, :]).astype(BF16)

    def proj(lo, hi):
        return _dot(u, win_ref[:, lo:hi])

    act_ref[rows, A_AQ:A_KD] = (proj(W_AQ, W_KV) * (HEAD_DIM ** -0.5)).astype(BF16)
    yield

    first = lax.broadcasted_iota(jnp.int32, (nrow // 2, KV_WIDTH), 1) < HEAD_DIM
    for part in range(2):
        prow = slice(rows.start + part * (nrow // 2), rows.start + (part + 1) * (nrow // 2))
        kv = _dot(u[part * (nrow // 2):(part + 1) * (nrow // 2), :], win_ref[:, W_KV:W_RQ])
        for src, dst in ((0, A_KD), (KV_WIDTH, A_VD)):
            t = kv[:, src:src + KV_WIDTH]
            ts = _swap_halves(t)
            act_ref[prow, dst:dst + KV_WIDTH] = jnp.where(first, t, ts).astype(BF16)
            act_ref[prow, dst + KV_WIDTH:dst + 2 * KV_WIDTH] = jnp.where(first, ts, t).astype(BF16)
    yield

    pos0 = pl.multiple_of((pl.program_id(0) % TILES_PER_SEQ) * TM + rows.start, nrow)
    cos = cos_ref[pl.ds(pos0, nrow), :]
    sin = sin_ref[pl.ds(pos0, nrow), :]
    for w0, a0, scale in ((W_RQ, A_RQ, 1.0), (W_RK, A_RK, RET_HEAD_DIM ** -0.5)):
        z = proj(w0, w0 + RET_WIDTH)
        for hd in range(N_RET_HEADS):
            zh = z[:, hd * RET_HEAD_DIM:(hd + 1) * RET_HEAD_DIM]
            zr = zh * cos + _swap_halves(zh) * sin
            if scale != 1.0:
                zr = zr * scale
            act_ref[rows, a0 + hd * RET_HEAD_DIM:a0 + (hd + 1) * RET_HEAD_DIM] = zr.astype(BF16)
        yield
    act_ref[rows, A_SG:D_ACT] = jax.nn.silu(proj(W_RG, D_IN)).astype(BF16)
    yield
    act_ref[rows, A_RV:A_SG] = proj(W_RV, W_RG).astype(BF16)
    yield


def _delayed(gen, n):
    for _ in range(n):
        yield
    yield from gen


def _pre_kernel(x_ref, n1_ref, wg_ref, wu_ref, wd_ref, n2_ref, win_ref, cos_ref, sin_ref, *rest):
    cast_src, (h_ref, act_ref), cast_dst, hmid_ref = rest[:4], rest[4:6], rest[6:10], rest[10]
    _cast_blocks(cast_src, cast_dst)
    args = (x_ref, n1_ref, wg_ref, wu_ref, wd_ref, n2_ref, win_ref, cos_ref, sin_ref, h_ref, act_ref, hmid_ref)
    half = TM // 2
    _interleave((_pre_units(slice(0, half), *args), N_PRE_UNITS),
                (_delayed(_pre_units(slice(half, TM), *args), PRE_SKEW), N_PRE_UNITS + PRE_SKEW))


def _mix_units(seq_start, consumed, sink_ref, decay_ref, act_ref, kvp_ref, dmat_ref, zeta_ref, xi_ref, gnw_ref,
               o_ref, state_ref):
    def writable(j):
        return (j * BLK) // (TM // 2) in consumed

    row = lax.broadcasted_iota(jnp.int32, (BLK, 2 * BLK), 0)
    col = lax.broadcasted_iota(jnp.int32, (BLK, 2 * BLK), 1)
    band = (col > row) & (col <= row + WINDOW)
    band0 = band & ((col >= BLK) | jnp.logical_not(seq_start))
    lo = lax.broadcasted_iota(jnp.int32, (BLK, 2 * HEAD_DIM), 1) < HEAD_DIM
    zero_bf = jnp.zeros((BLK, 2 * HEAD_DIM), BF16)
    pair_w = 2 * HEAD_DIM

    def attention(j, kvh):
        r0 = j * BLK
        rows = slice(r0, r0 + BLK)
        valid = band0 if j == 0 else band
        kc = A_KD + kvh * pair_w
        vc = A_VD + kvh * pair_w
        if j == 0:
            k_h = jnp.concatenate([kvp_ref[:, kc - A_KD:kc - A_KD + pair_w], act_ref[0:BLK, kc:kc + pair_w]], axis=0)
            v_h = jnp.concatenate([kvp_ref[:, vc - A_KD:vc - A_KD + pair_w], act_ref[0:BLK, vc:vc + pair_w]], axis=0)
        else:
            k_h = act_ref[r0 - BLK:r0 + BLK, kc:kc + pair_w]
            v_h = act_ref[r0 - BLK:r0 + BLK, vc:vc + pair_w]
        qs = []
        for p in range(GQA_GROUP // 2):
            c0 = A_AQ + (kvh * 2 + p) * pair_w
            qp = act_ref[rows, c0:c0 + pair_w]
            qs.append(jnp.where(lo, qp, zero_bf))
            qs.append(jnp.where(lo, zero_bf, qp))
        s_all = _dot_nt(jnp.concatenate(qs, axis=0), k_h)
        yield
        ps, invs = [], []
        for g in range(GQA_GROUP):
            sink = sink_ref[kvh * GQA_GROUP + g]
            s = jnp.where(valid, s_all[g * BLK:(g + 1) * BLK, :], NEG_INF)
            m = jnp.maximum(jnp.max(s, axis=-1, keepdims=True), sink)
            e = jnp.exp(s - m)
            den = jnp.sum(e, axis=-1, keepdims=True) + jnp.exp(sink - m)
            ps.append(e.astype(BF16))
            invs.append(1.0 / den)
        o_all = _dot(jnp.concatenate(ps, axis=0), v_h)
        yield
        assert writable(j), "mixing output traced before the previous tile's rows were read"
        for p in range(GQA_GROUP // 2):
            c0 = (kvh * 2 + p) * pair_w
            oe = o_all[(2 * p) * BLK:(2 * p + 1) * BLK, :] * invs[2 * p]
            oo = o_all[(2 * p + 1) * BLK:(2 * p + 2) * BLK, :] * invs[2 * p + 1]
            o_ref[rows, c0:c0 + pair_w] = jnp.where(lo, oe, oo).astype(BF16)

    def retention(j, heads):
        rows = slice(j * BLK, (j + 1) * BLK)
        part = []
        for hd in heads:
            c = hd * RET_HEAD_DIM
            q = act_ref[rows, A_RQ + c:A_RQ + c + RET_HEAD_DIM]
            k = act_ref[rows, A_RK + c:A_RK + c + RET_HEAD_DIM]
            v = act_ref[rows, A_RV + c:A_RV + c + RET_HEAD_DIM]
            vz = (v.astype(F32) * zeta_ref[hd]).astype(BF16)
            part.append((v, _dot_nt(q, k), _dot(q, state_ref[hd].astype(BF16)), _dot_tn(k, vz)))
        yield
        intra = []
        for hd, (v, sc, _, kv) in zip(heads, part, strict=True):
            state_ref[hd] = state_ref[hd] * decay_ref[hd] + kv
            intra.append(_dot((sc * dmat_ref[hd]).astype(BF16), v))
        yield
        assert writable(j), "mixing output traced before the previous tile's rows were read"
        for hd, (_, _, cross, _), y_intra in zip(heads, part, intra, strict=True):
            c = hd * RET_HEAD_DIM
            y = y_intra + cross * xi_ref[hd]
            mu = jnp.mean(y, axis=-1, keepdims=True)
            d = y - mu
            var = jnp.mean(d * d, axis=-1, keepdims=True)
            yn = d * lax.rsqrt(var + GN_EPS) * gnw_ref[0:1, c:c + RET_HEAD_DIM]
            sg = act_ref[rows, A_SG + c:A_SG + c + RET_HEAD_DIM]
            o_ref[rows, ATTN_WIDTH + c:ATTN_WIDTH + c + RET_HEAD_DIM] = (sg.astype(F32) * yn).astype(BF16)

    chains = []
    for j in range(BLKS_PER_TILE):
        chains += [attention(j, 0), attention(j, 1), retention(j, (0, 1)), retention(j, (2, 3))]
    active = []
    for chain in chains + [None] * (MIX_STAGES - 1):
        if chain is not None:
            active.append(chain)
        for g in list(active):
            try:
                next(g)
            except StopIteration:
                active.remove(g)
        yield


def _post_units(final, rows, consumed, ar_ref, h_ref, wo_ref, n_ref, wg_ref, wu_ref, wd_ref, fn_ref, o_ref,
                hmid_ref):
    h2 = h_ref[rows, :] + _dot(ar_ref[rows, :], wo_ref[...])
    consumed.add(rows.start // (rows.stop - rows.start))
    o_ref[rows, :] = h2
    xn = _rms(h2, n_ref[0:1, :]).astype(BF16)
    yield
    for c in range(D_FF // FF_CHUNK):
        sl = slice(c * FF_CHUNK, (c + 1) * FF_CHUNK)
        g = _dot(xn, wg_ref[:, sl])
        u = _dot(xn, wu_ref[:, sl])
        hmid_ref[rows, sl] = (jax.nn.silu(g) * u).astype(BF16)
        yield
    for c in range(D_MODEL // OUT_CHUNK):
        sl = slice(c * OUT_CHUNK, (c + 1) * OUT_CHUNK)
        o_ref[rows, sl] = o_ref[rows, sl] + 0.5 * _dot(hmid_ref[rows, :], wd_ref[:, sl])
        yield
    if final:
        o_ref[rows, :] = _rms(o_ref[rows, :], fn_ref[0:1, :])


def _interleave(*streams, early=()):
    total = max(n for _, n in streams)
    done = [0] * len(streams)
    for k in range(1, total + 1):
        for idx, (gen, n) in enumerate(streams):
            span = total - EARLY_ROUNDS if idx in early else total
            while done[idx] < -(-min(k, span) * n // span):
                next(gen)
                done[idx] += 1
    for gen, _ in streams:
        for _ in gen:
            raise AssertionError("stream yielded more often than declared")


def _mixpost_kernel(final, sink_ref, decay_ref, act_ref, kvp_ref, dmat_ref, zeta_ref, xi_ref, gnw_ref,
                    h_ref, wo_ref, n_ref, wg_ref, wu_ref, wd_ref, fn_ref, *rest):
    n_cast = 0 if final else 4
    cast_src, o_ref, cast_dst = rest[:n_cast], rest[n_cast], rest[n_cast + 1:2 * n_cast + 1]
    state_ref, ar_ref, hmid_ref = rest[2 * n_cast + 1:]
    t = pl.program_id(0)
    seq_start = (jnp.minimum(t, N_TILES - 1) % TILES_PER_SEQ) == 0

    @pl.when(seq_start)
    def _():
        state_ref[...] = jnp.zeros_like(state_ref)

    half = TM // 2

    def post_streams(consumed):
        args = (consumed, ar_ref, h_ref, wo_ref, n_ref, wg_ref, wu_ref, wd_ref, fn_ref, o_ref, hmid_ref)
        return ((_post_units(final, slice(0, half), *args), N_POST_UNITS),
                (_delayed(_post_units(final, slice(half, TM), *args), POST_SKEW), N_POST_UNITS + POST_SKEW))

    def mix_stream(consumed):
        return (_delayed(_mix_units(seq_start, consumed, sink_ref, decay_ref, act_ref, kvp_ref, dmat_ref, zeta_ref,
                                    xi_ref, gnw_ref, ar_ref, state_ref), MIX_DELAY), N_MIX_SLOTS + MIX_DELAY)

    @pl.when(t == 0)
    def _():
        _cast_blocks(cast_src, cast_dst)
        _interleave(mix_stream({0, 1}))

    @pl.when((t > 0) & (t < N_TILES))
    def _():
        _cast_blocks(cast_src, cast_dst)
        consumed = set()
        _interleave(*post_streams(consumed), mix_stream(consumed), early=(2,))

    @pl.when(t == N_TILES)
    def _():
        _cast_blocks(cast_src, cast_dst)
        _interleave(*post_streams(set()))


def _resident(shape, layer):
    nd = len(shape)
    return pl.BlockSpec((None,) + shape, lambda *_: (layer,) + (0,) * nd, pipeline_mode=pl.Buffered(1))


def _const(shape):
    nd = len(shape)
    return pl.BlockSpec(shape, lambda *_: (0,) * nd, pipeline_mode=pl.Buffered(1))


def _cast_specs(layer, shapes):
    ins, outs, shapes_out = [], [], []
    for rows, cols in shapes:
        blk = CAST_ROWS[rows]
        last = rows // blk - 1
        ins.append(pl.BlockSpec((None, blk, cols), lambda i, last=last: (layer, jnp.minimum(i, last), 0)))
        outs.append(pl.BlockSpec((blk, cols), lambda i, last=last: (jnp.minimum(i, last), 0)))
        shapes_out.append(jax.ShapeDtypeStruct((rows, cols), BF16))
    return ins, outs, shapes_out


def _pre_call(layer, x, n1, wg, wu, wd, n2, win, cos_t, sin_t, next_f32):
    tok = lambda w: pl.BlockSpec((TM, w), lambda i: (i, 0))
    cast_in, cast_out, cast_shapes = _cast_specs(
        layer, ((D_MODEL, D_MODEL), (D_MODEL, D_FF), (D_MODEL, D_FF), (D_FF, D_MODEL)))
    return pl.pallas_call(
        _pre_kernel,
        grid=(N_TILES,),
        in_specs=[
            tok(D_MODEL),
            _resident((GAIN_ROWS, D_MODEL), layer),
            _const((D_MODEL, D_FF)),
            _const((D_MODEL, D_FF)),
            _const((D_FF, D_MODEL)),
            _resident((GAIN_ROWS, D_MODEL), layer),
            _const((D_MODEL, D_IN)),
            _const((SEQ, RET_HEAD_DIM)),
            _const((SEQ, RET_HEAD_DIM)),
        ] + cast_in,
        out_specs=[tok(D_MODEL), tok(D_ACT)] + cast_out,
        out_shape=[jax.ShapeDtypeStruct((N_TOK, D_MODEL), F32),
                   jax.ShapeDtypeStruct((N_TOK, D_ACT), BF16)] + cast_shapes,
        scratch_shapes=[pltpu.VMEM((TM, D_FF), BF16)],
        compiler_params=pltpu.CompilerParams(dimension_semantics=("arbitrary",), vmem_limit_bytes=VMEM_LIMIT),
        name=f"pre{layer}",
    )(x, n1, wg, wu, wd, n2, win, cos_t, sin_t, *next_f32)


def _mixpost_call(layer, final, sinks, decay, act, dmat, zeta_b, xi_b, gnw, h, wo, n, wg, wu, wd, fn, next_f32):
    def mix_tile(i):
        return jnp.minimum(i, N_TILES - 1)

    if final:
        cast_in, cast_out, cast_shapes = [], [], []
    else:
        cast_in, cast_out, cast_shapes = _cast_specs(
            layer + 1, ((D_MODEL, D_FF), (D_MODEL, D_FF), (D_FF, D_MODEL), (D_MODEL, D_IN)))

    prev_kv = pl.BlockSpec((BLK, A_RQ - A_KD),
                           lambda i: (jnp.maximum(mix_tile(i) * BLKS_PER_TILE - 1, 0), A_KD // (A_RQ - A_KD)))
    lag = pl.BlockSpec((TM, D_MODEL), lambda i: (jnp.maximum(i - 1, 0), 0))
    smem = pl.BlockSpec(memory_space=pltpu.SMEM)
    return pl.pallas_call(
        functools.partial(_mixpost_kernel, final),
        grid=(N_TILES + 1,),
        in_specs=[smem, smem,
                  pl.BlockSpec((TM, D_ACT), lambda i: (mix_tile(i), 0)),
                  prev_kv,
                  _const((N_RET_HEADS, BLK, BLK)), _const((N_RET_HEADS, BLK, BLK)), _const((N_RET_HEADS, BLK, BLK)),
                  _resident((GAIN_ROWS, RET_WIDTH), layer),
                  lag,
                  _const((D_MODEL, D_MODEL)),
                  _resident((GAIN_ROWS, D_MODEL), layer),
                  _const((D_MODEL, D_FF)),
                  _const((D_MODEL, D_FF)),
                  _const((D_FF, D_MODEL)),
                  _const((GAIN_ROWS, D_MODEL))] + cast_in,
        out_specs=[lag] + cast_out,
        out_shape=[jax.ShapeDtypeStruct((N_TOK, D_MODEL), F32)] + cast_shapes,
        scratch_shapes=[pltpu.VMEM((N_RET_HEADS, RET_HEAD_DIM, RET_HEAD_DIM), F32),
                        pltpu.VMEM((TM, D_MODEL), BF16),
                        pltpu.VMEM((TM, D_FF), BF16)],
        compiler_params=pltpu.CompilerParams(dimension_semantics=("arbitrary",), vmem_limit_bytes=VMEM_LIMIT),
        name=f"mixpost{layer}",
    )(sinks, decay, act, act, dmat, zeta_b, xi_b, gnw, h, wo, n, wg, wu, wd, fn, *next_f32)


def _tables():
    pos = jnp.arange(SEQ, dtype=F32)
    inv_freq = ROPE_BASE ** (-jnp.arange(0, RET_HEAD_DIM, 2, dtype=F32) / RET_HEAD_DIM)
    ang = pos[:, None] * inv_freq[None, :]
    cos, sin = jnp.cos(ang), jnp.sin(ang)
    cos_t = jnp.concatenate([cos, cos], axis=-1)
    sin_t = jnp.concatenate([-sin, sin], axis=-1)
    log_gamma = jnp.log(1.0 - 2.0 ** (-5.0 - jnp.arange(N_RET_HEADS, dtype=F32)))
    idx = jnp.arange(BLK, dtype=F32)
    dif = idx[:, None] - idx[None, :]
    dmat = jnp.where(dif[None] >= 0, jnp.exp(jnp.maximum(dif, 0.0)[None] * log_gamma[:, None, None]), 0.0)
    zeta = jnp.exp((BLK - 1.0 - idx)[None, :] * log_gamma[:, None])
    xi = jnp.exp((idx + 1.0)[None, :] * log_gamma[:, None])
    decay = jnp.exp(BLK * log_gamma)
    bshape = (N_RET_HEADS, BLK, RET_HEAD_DIM)
    zeta_b = jnp.broadcast_to(zeta[:, :, None], bshape)
    xi_b = jnp.broadcast_to(xi[:, :, None], bshape)
    return cos_t, sin_t, dmat, zeta_b, xi_b, decay


def kernel(x, ffn1_norm, ffn1_w_gate, ffn1_w_up, ffn1_w_down, mix_norm, w_in, attn_sinks, ret_gn_w, w_out,
           ffn2_norm, ffn2_w_gate, ffn2_w_up, ffn2_w_down, final_norm):
    cos_t, sin_t, dmat, zeta_b, xi_b, decay = _tables()
    pre_w = [w[0].astype(BF16) for w in (ffn1_w_gate, ffn1_w_up, ffn1_w_down, w_in)]
    pre_f32 = (ffn1_w_gate, ffn1_w_up, ffn1_w_down, w_in)
    post_f32 = (w_out, ffn2_w_gate, ffn2_w_up, ffn2_w_down)
    def rows8(v, width):
        return jnp.broadcast_to(v.reshape(-1, 1, width), (v.size // width, GAIN_ROWS, width))

    n1, nm, n2 = rows8(ffn1_norm, D_MODEL), rows8(mix_norm, D_MODEL), rows8(ffn2_norm, D_MODEL)
    gnw = rows8(ret_gn_w, RET_WIDTH)
    fn = rows8(final_norm, D_MODEL)[0]

    h = x.reshape(N_TOK, D_MODEL)
    for layer in range(DEPTH):
        final = layer == DEPTH - 1
        wg1, wu1, wd1, win = pre_w
        h, act, wo, wg2, wu2, wd2 = _pre_call(layer, h, n1, wg1, wu1, wd1, nm, win, cos_t, sin_t, post_f32)
        h, *pre_w = _mixpost_call(layer, final, attn_sinks[layer], decay, act, dmat, zeta_b, xi_b, gnw,
                                  h, wo, n2, wg2, wu2, wd2, fn, () if final else pre_f32)
    return h.reshape(BATCH, SEQ, D_MODEL)
```

```python
import functools

import jax
import jax.numpy as jnp
from jax import lax
from jax.experimental import pallas as pl
from jax.experimental.pallas import tpu as pltpu

D_MODEL = 1024
BATCH = 8
SEQ = 2048
DEPTH = 4
ATTN_WIDTH = 512
RET_WIDTH = 512
HEAD_DIM = 64
N_KV_HEADS = 2
GQA_GROUP = 4
KV_WIDTH = 128
WINDOW = 128
BLK = 128
N_RET_HEADS = 4
RET_HEAD_DIM = 128
ROPE_BASE = 10000.0
D_FF = 2816
D_IN = 2816
NORM_EPS = 1e-6
GN_EPS = 1e-5
NEG_INF = -1e30
N_TOK = BATCH * SEQ

W_AQ, W_KV, W_RQ, W_RK, W_RV, W_RG = 0, 512, 768, 1280, 1792, 2304
A_AQ, A_KD, A_VD, A_RQ, A_RK, A_RV, A_SG, D_ACT = 0, 512, 768, 1024, 1536, 2048, 2560, 3072

TM = 512
N_TILES = N_TOK // TM
TILES_PER_SEQ = SEQ // TM
BLKS_PER_TILE = TM // BLK
FF_CHUNK = 256
OUT_CHUNK = 512
N_PRE_UNITS = 1 + D_FF // FF_CHUNK + D_MODEL // OUT_CHUNK + 6
PRE_SKEW = 3
MIX_STAGES = 3
N_MIX_SLOTS = 4 * BLKS_PER_TILE + MIX_STAGES - 1
N_POST_UNITS = 1 + D_FF // FF_CHUNK + D_MODEL // OUT_CHUNK
POST_SKEW = 2
MIX_DELAY = 1
EARLY_ROUNDS = 1
CAST_ROWS = {D_MODEL: D_MODEL // N_TILES, D_FF: D_FF // (N_TILES // 2)}
VMEM_LIMIT = 56 * 1024 * 1024
GAIN_ROWS = 8

F32 = jnp.float32
BF16 = jnp.bfloat16


def _dot(a, b):
    return jnp.dot(a, b, preferred_element_type=F32)


def _dot_nt(a, b):
    return lax.dot_general(a, b, (((1,), (1,)), ((), ())), preferred_element_type=F32)


def _dot_tn(a, b):
    return lax.dot_general(a, b, (((0,), (0,)), ((), ())), preferred_element_type=F32)


def _rms(x, w):
    y = x * lax.rsqrt(jnp.mean(x * x, axis=-1, keepdims=True) + NORM_EPS)
    return y * w


def _swap_halves(x):
    return pltpu.roll(x, x.shape[-1] // 2, 1)


def _cast_blocks(src_refs, dst_refs):
    for src, dst in zip(src_refs, dst_refs, strict=True):
        dst[...] = src[...].astype(BF16)


def _pre_units(rows, x_ref, n1_ref, wg_ref, wu_ref, wd_ref, n2_ref, win_ref, cos_ref, sin_ref,
               h_ref, act_ref, hmid_ref):
    nrow = rows.stop - rows.start
    x = x_ref[rows, :]
    xw = (x * n1_ref[0:1, :]).astype(BF16)
    r = jnp.broadcast_to(lax.rsqrt(jnp.mean(x * x, axis=-1, keepdims=True) + NORM_EPS), (nrow, FF_CHUNK))
    yield
    for c in range(D_FF // FF_CHUNK):
        sl = slice(c * FF_CHUNK, (c + 1) * FF_CHUNK)
        g = _dot(xw, wg_ref[:, sl]) * r
        u = _dot(xw, wu_ref[:, sl]) * r
        hmid_ref[rows, sl] = (jax.nn.silu(g) * u).astype(BF16)
        yield
    for c in range(D_MODEL // OUT_CHUNK):
        sl = slice(c * OUT_CHUNK, (c + 1) * OUT_CHUNK)
        h_ref[rows, sl] = x_ref[rows, sl] + 0.5 * _dot(hmid_ref[rows, :], wd_ref[:, sl])
        yield
    u = _rms(h_ref[rows, :], n2_ref[0:1, :]).astype(BF16)

    def proj(lo, hi):
        return _dot(u, win_ref[:, lo:hi])

    act_ref[rows, A_AQ:A_KD] = (proj(W_AQ, W_KV) * (HEAD_DIM ** -0.5)).astype(BF16)
    yield

    first = lax.broadcasted_iota(jnp.int32, (nrow // 2, KV_WIDTH), 1) < HEAD_DIM
    for part in range(2):
        prow = slice(rows.start + part * (nrow // 2), rows.start + (part + 1) * (nrow // 2))
        kv = _dot(u[part * (nrow // 2):(part + 1) * (nrow // 2), :], win_ref[:, W_KV:W_RQ])
        for src, dst in ((0, A_KD), (KV_WIDTH, A_VD)):
            t = kv[:, src:src + KV_WIDTH]
            ts = _swap_halves(t)
            act_ref[prow, dst:dst + KV_WIDTH] = jnp.where(first, t, ts).astype(BF16)
            act_ref[prow, dst + KV_WIDTH:dst + 2 * KV_WIDTH] = jnp.where(first, ts, t).astype(BF16)
    yield

    pos0 = pl.multiple_of((pl.program_id(0) % TILES_PER_SEQ) * TM + rows.start, nrow)
    cos = cos_ref[pl.ds(pos0, nrow), :]
    sin = sin_ref[pl.ds(pos0, nrow), :]
    for w0, a0, scale in ((W_RQ, A_RQ, 1.0), (W_RK, A_RK, RET_HEAD_DIM ** -0.5)):
        z = proj(w0, w0 + RET_WIDTH)
        for hd in range(N_RET_HEADS):
            zh = z[:, hd * RET_HEAD_DIM:(hd + 1) * RET_HEAD_DIM]
            zr = zh * cos + _swap_halves(zh) * sin
            if scale != 1.0:
                zr = zr * scale
            act_ref[rows, a0 + hd * RET_HEAD_DIM:a0 + (hd + 1) * RET_HEAD_DIM] = zr.astype(BF16)
        yield
    act_ref[rows, A_SG:D_ACT] = jax.nn.silu(proj(W_RG, D_IN)).astype(BF16)
    yield
    act_ref[rows, A_RV:A_SG] = proj(W_RV, W_RG).astype(BF16)
    yield


def _delayed(gen, n):
    for _ in range(n):
        yield
    yield from gen


def _pre_kernel(x_ref, n1_ref, wg_ref, wu_ref, wd_ref, n2_ref, win_ref, cos_ref, sin_ref, *rest):
    cast_src, (h_ref, act_ref), cast_dst, hmid_ref = rest[:4], rest[4:6], rest[6:10], rest[10]
    _cast_blocks(cast_src, cast_dst)
    args = (x_ref, n1_ref, wg_ref, wu_ref, wd_ref, n2_ref, win_ref, cos_ref, sin_ref, h_ref, act_ref, hmid_ref)
    half = TM // 2
    _interleave((_pre_units(slice(0, half), *args), N_PRE_UNITS),
                (_delayed(_pre_units(slice(half, TM), *args), PRE_SKEW), N_PRE_UNITS + PRE_SKEW))


def _mix_units(seq_start, consumed, sink_ref, decay_ref, act_ref, kvp_ref, dmat_ref, zeta_ref, xi_ref, gnw_ref,
               o_ref, state_ref):
    def writable(j):
        return (j * BLK) // (TM // 2) in consumed

    row = lax.broadcasted_iota(jnp.int32, (BLK, 2 * BLK), 0)
    col = lax.broadcasted_iota(jnp.int32, (BLK, 2 * BLK), 1)
    band = (col > row) & (col <= row + WINDOW)
    band0 = band & ((col >= BLK) | jnp.logical_not(seq_start))
    lo = lax.broadcasted_iota(jnp.int32, (BLK, 2 * HEAD_DIM), 1) < HEAD_DIM
    zero_bf = jnp.zeros((BLK, 2 * HEAD_DIM), BF16)
    pair_w = 2 * HEAD_DIM

    def attention(j, kvh):
        r0 = j * BLK
        rows = slice(r0, r0 + BLK)
        valid = band0 if j == 0 else band
        kc = A_KD + kvh * pair_w
        vc = A_VD + kvh * pair_w
        if j == 0:
            k_h = jnp.concatenate([kvp_ref[:, kc - A_KD:kc - A_KD + pair_w], act_ref[0:BLK, kc:kc + pair_w]], axis=0)
            v_h = jnp.concatenate([kvp_ref[:, vc - A_KD:vc - A_KD + pair_w], act_ref[0:BLK, vc:vc + pair_w]], axis=0)
        else:
            k_h = act_ref[r0 - BLK:r0 + BLK, kc:kc + pair_w]
            v_h = act_ref[r0 - BLK:r0 + BLK, vc:vc + pair_w]
        qs = []
        for p in range(GQA_GROUP // 2):
            c0 = A_AQ + (kvh * 2 + p) * pair_w
            qp = act_ref[rows, c0:c0 + pair_w]
            qs.append(jnp.where(lo, qp, zero_bf))
            qs.append(jnp.where(lo, zero_bf, qp))
        s_all = _dot_nt(jnp.concatenate(qs, axis=0), k_h)
        yield
        ps, invs = [], []
        for g in range(GQA_GROUP):
            sink = sink_ref[kvh * GQA_GROUP + g]
            s = jnp.where(valid, s_all[g * BLK:(g + 1) * BLK, :], NEG_INF)
            m = jnp.maximum(jnp.max(s, axis=-1, keepdims=True), sink)
            e = jnp.exp(s - m)
            den = jnp.sum(e, axis=-1, keepdims=True) + jnp.exp(sink - m)
            ps.append(e.astype(BF16))
            invs.append(1.0 / den)
        o_all = _dot(jnp.concatenate(ps, axis=0), v_h)
        yield
        assert writable(j), "mixing output traced before the previous tile's rows were read"
        for p in range(GQA_GROUP // 2):
            c0 = (kvh * 2 + p) * pair_w
            oe = o_all[(2 * p) * BLK:(2 * p + 1) * BLK, :] * invs[2 * p]
            oo = o_all[(2 * p + 1) * BLK:(2 * p + 2) * BLK, :] * invs[2 * p + 1]
            o_ref[rows, c0:c0 + pair_w] = jnp.where(lo, oe, oo).astype(BF16)

    def block_diag(a, b):
        zero = jnp.zeros_like(a)
        return jnp.concatenate([jnp.concatenate([a, zero], axis=1), jnp.concatenate([zero, b], axis=1)], axis=0)

    def retention(j, heads):
        rows = slice(j * BLK, (j + 1) * BLK)
        d = RET_HEAD_DIM
        c0 = heads[0] * d
        q2 = act_ref[rows, A_RQ + c0:A_RQ + c0 + 2 * d]
        k2 = act_ref[rows, A_RK + c0:A_RK + c0 + 2 * d]
        v2 = act_ref[rows, A_RV + c0:A_RV + c0 + 2 * d]
        sc2 = _dot_nt(q2, block_diag(k2[:, :d], k2[:, d:]))
        cross2 = _dot(q2, block_diag(state_ref[heads[0]].astype(BF16), state_ref[heads[1]].astype(BF16)))
        kvs = []
        for i, hd in enumerate(heads):
            vz = (v2[:, i * d:(i + 1) * d].astype(F32) * zeta_ref[hd]).astype(BF16)
            kvs.append(_dot_tn(k2[:, i * d:(i + 1) * d], vz))
        yield
        for hd, kv in zip(heads, kvs, strict=True):
            state_ref[hd] = state_ref[hd] * decay_ref[hd] + kv
        sc = jnp.concatenate([sc2[:, i * d:(i + 1) * d] * dmat_ref[hd] for i, hd in enumerate(heads)], axis=1)
        intra2 = _dot(sc.astype(BF16), block_diag(v2[:, :d], v2[:, d:]))
        yield
        assert writable(j), "mixing output traced before the previous tile's rows were read"
        for i, hd in enumerate(heads):
            c = hd * RET_HEAD_DIM
            y = intra2[:, i * d:(i + 1) * d] + cross2[:, i * d:(i + 1) * d] * xi_ref[hd]
            mu = jnp.mean(y, axis=-1, keepdims=True)
            dev = y - mu
            var = jnp.mean(dev * dev, axis=-1, keepdims=True)
            yn = dev * lax.rsqrt(var + GN_EPS) * gnw_ref[0:1, c:c + RET_HEAD_DIM]
            sg = act_ref[rows, A_SG + c:A_SG + c + RET_HEAD_DIM]
            o_ref[rows, ATTN_WIDTH + c:ATTN_WIDTH + c + RET_HEAD_DIM] = (sg.astype(F32) * yn).astype(BF16)

    chains = []
    for j in range(BLKS_PER_TILE):
        chains += [attention(j, 0), attention(j, 1), retention(j, (0, 1)), retention(j, (2, 3))]
    active = []
    for chain in chains + [None] * (MIX_STAGES - 1):
        if chain is not None:
            active.append(chain)
        for g in list(active):
            try:
                next(g)
            except StopIteration:
                active.remove(g)
        yield


def _post_units(final, rows, consumed, ar_ref, h_ref, wo_ref, n_ref, wg_ref, wu_ref, wd_ref, fn_ref, o_ref,
                hmid_ref):
    h2 = h_ref[rows, :] + _dot(ar_ref[rows, :], wo_ref[...])
    consumed.add(rows.start // (rows.stop - rows.start))
    o_ref[rows, :] = h2
    xn = _rms(h2, n_ref[0:1, :]).astype(BF16)
    yield
    for c in range(D_FF // FF_CHUNK):
        sl = slice(c * FF_CHUNK, (c + 1) * FF_CHUNK)
        g = _dot(xn, wg_ref[:, sl])
        u = _dot(xn, wu_ref[:, sl])
        hmid_ref[rows, sl] = (jax.nn.silu(g) * u).astype(BF16)
        yield
    for c in range(D_MODEL // OUT_CHUNK):
        sl = slice(c * OUT_CHUNK, (c + 1) * OUT_CHUNK)
        o_ref[rows, sl] = o_ref[rows, sl] + 0.5 * _dot(hmid_ref[rows, :], wd_ref[:, sl])
        yield
    if final:
        o_ref[rows, :] = _rms(o_ref[rows, :], fn_ref[0:1, :])


def _interleave(*streams, early=()):
    total = max(n for _, n in streams)
    done = [0] * len(streams)
    for k in range(1, total + 1):
        for idx, (gen, n) in enumerate(streams):
            span = total - EARLY_ROUNDS if idx in early else total
            while done[idx] < -(-min(k, span) * n // span):
                next(gen)
                done[idx] += 1
    for gen, _ in streams:
        for _ in gen:
            raise AssertionError("stream yielded more often than declared")


def _mixpost_kernel(final, sink_ref, decay_ref, act_ref, kvp_ref, dmat_ref, zeta_ref, xi_ref, gnw_ref,
                    h_ref, wo_ref, n_ref, wg_ref, wu_ref, wd_ref, fn_ref, *rest):
    n_cast = 0 if final else 4
    cast_src, o_ref, cast_dst = rest[:n_cast], rest[n_cast], rest[n_cast + 1:2 * n_cast + 1]
    state_ref, ar_ref, hmid_ref = rest[2 * n_cast + 1:]
    t = pl.program_id(0)
    seq_start = (jnp.minimum(t, N_TILES - 1) % TILES_PER_SEQ) == 0

    @pl.when(seq_start)
    def _():
        state_ref[...] = jnp.zeros_like(state_ref)

    half = TM // 2

    def post_streams(consumed):
        args = (consumed, ar_ref, h_ref, wo_ref, n_ref, wg_ref, wu_ref, wd_ref, fn_ref, o_ref, hmid_ref)
        return ((_post_units(final, slice(0, half), *args), N_POST_UNITS),
                (_delayed(_post_units(final, slice(half, TM), *args), POST_SKEW), N_POST_UNITS + POST_SKEW))

    def mix_stream(consumed):
        return (_delayed(_mix_units(seq_start, consumed, sink_ref, decay_ref, act_ref, kvp_ref, dmat_ref, zeta_ref,
                                    xi_ref, gnw_ref, ar_ref, state_ref), MIX_DELAY), N_MIX_SLOTS + MIX_DELAY)

    @pl.when(t == 0)
    def _():
        _cast_blocks(cast_src, cast_dst)
        _interleave(mix_stream({0, 1}))

    @pl.when((t > 0) & (t < N_TILES))
    def _():
        _cast_blocks(cast_src, cast_dst)
        consumed = set()
        _interleave(*post_streams(consumed), mix_stream(consumed), early=(2,))

    @pl.when(t == N_TILES)
    def _():
        _cast_blocks(cast_src, cast_dst)
        _interleave(*post_streams(set()))


def _resident(shape, layer):
    nd = len(shape)
    return pl.BlockSpec((None,) + shape, lambda *_: (layer,) + (0,) * nd, pipeline_mode=pl.Buffered(1))


def _const(shape):
    nd = len(shape)
    return pl.BlockSpec(shape, lambda *_: (0,) * nd, pipeline_mode=pl.Buffered(1))


def _cast_specs(layer, shapes):
    ins, outs, shapes_out = [], [], []
    for rows, cols in shapes:
        blk = CAST_ROWS[rows]
        last = rows // blk - 1
        ins.append(pl.BlockSpec((None, blk, cols), lambda i, last=last: (layer, jnp.minimum(i, last), 0)))
        outs.append(pl.BlockSpec((blk, cols), lambda i, last=last: (jnp.minimum(i, last), 0)))
        shapes_out.append(jax.ShapeDtypeStruct((rows, cols), BF16))
    return ins, outs, shapes_out


def _pre_call(layer, x, n1, wg, wu, wd, n2, win, cos_t, sin_t, next_f32):
    tok = lambda w: pl.BlockSpec((TM, w), lambda i: (i, 0))
    cast_in, cast_out, cast_shapes = _cast_specs(
        layer, ((D_MODEL, D_MODEL), (D_MODEL, D_FF), (D_MODEL, D_FF), (D_FF, D_MODEL)))
    return pl.pallas_call(
        _pre_kernel,
        grid=(N_TILES,),
        in_specs=[
            tok(D_MODEL),
            _resident((GAIN_ROWS, D_MODEL), layer),
            _const((D_MODEL, D_FF)),
            _const((D_MODEL, D_FF)),
            _const((D_FF, D_MODEL)),
            _resident((GAIN_ROWS, D_MODEL), layer),
            _const((D_MODEL, D_IN)),
            _const((SEQ, RET_HEAD_DIM)),
            _const((SEQ, RET_HEAD_DIM)),
        ] + cast_in,
        out_specs=[tok(D_MODEL), tok(D_ACT)] + cast_out,
        out_shape=[jax.ShapeDtypeStruct((N_TOK, D_MODEL), F32),
                   jax.ShapeDtypeStruct((N_TOK, D_ACT), BF16)] + cast_shapes,
        scratch_shapes=[pltpu.VMEM((TM, D_FF), BF16)],
        compiler_params=pltpu.CompilerParams(dimension_semantics=("arbitrary",), vmem_limit_bytes=VMEM_LIMIT),
        name=f"pre{layer}",
    )(x, n1, wg, wu, wd, n2, win, cos_t, sin_t, *next_f32)


def _mixpost_call(layer, final, sinks, decay, act, dmat, zeta_b, xi_b, gnw, h, wo, n, wg, wu, wd, fn, next_f32):
    def mix_tile(i):
        return jnp.minimum(i, N_TILES - 1)

    if final:
        cast_in, cast_out, cast_shapes = [], [], []
    else:
        cast_in, cast_out, cast_shapes = _cast_specs(
            layer + 1, ((D_MODEL, D_FF), (D_MODEL, D_FF), (D_FF, D_MODEL), (D_MODEL, D_IN)))

    prev_kv = pl.BlockSpec((BLK, A_RQ - A_KD),
                           lambda i: (jnp.maximum(mix_tile(i) * BLKS_PER_TILE - 1, 0), A_KD // (A_RQ - A_KD)))
    lag = pl.BlockSpec((TM, D_MODEL), lambda i: (jnp.maximum(i - 1, 0), 0))
    smem = pl.BlockSpec(memory_space=pltpu.SMEM)
    return pl.pallas_call(
        functools.partial(_mixpost_kernel, final),
        grid=(N_TILES + 1,),
        in_specs=[smem, smem,
                  pl.BlockSpec((TM, D_ACT), lambda i: (mix_tile(i), 0)),
                  prev_kv,
                  _const((N_RET_HEADS, BLK, BLK)), _const((N_RET_HEADS, BLK, BLK)), _const((N_RET_HEADS, BLK, BLK)),
                  _resident((GAIN_ROWS, RET_WIDTH), layer),
                  lag,
                  _const((D_MODEL, D_MODEL)),
                  _resident((GAIN_ROWS, D_MODEL), layer),
                  _const((D_MODEL, D_FF)),
                  _const((D_MODEL, D_FF)),
                  _const((D_FF, D_MODEL)),
                  _const((GAIN_ROWS, D_MODEL))] + cast_in,
        out_specs=[lag] + cast_out,
        out_shape=[jax.ShapeDtypeStruct((N_TOK, D_MODEL), F32)] + cast_shapes,
        scratch_shapes=[pltpu.VMEM((N_RET_HEADS, RET_HEAD_DIM, RET_HEAD_DIM), F32),
                        pltpu.VMEM((TM, D_MODEL), BF16),
                        pltpu.VMEM((TM, D_FF), BF16)],
        compiler_params=pltpu.CompilerParams(dimension_semantics=("arbitrary",), vmem_limit_bytes=VMEM_LIMIT),
        name=f"mixpost{layer}",
    )(sinks, decay, act, act, dmat, zeta_b, xi_b, gnw, h, wo, n, wg, wu, wd, fn, *next_f32)


def _tables():
    pos = jnp.arange(SEQ, dtype=F32)
    inv_freq = ROPE_BASE ** (-jnp.arange(0, RET_HEAD_DIM, 2, dtype=F32) / RET_HEAD_DIM)
    ang = pos[:, None] * inv_freq[None, :]
    cos, sin = jnp.cos(ang), jnp.sin(ang)
    cos_t = jnp.concatenate([cos, cos], axis=-1)
    sin_t = jnp.concatenate([-sin, sin], axis=-1)
    log_gamma = jnp.log(1.0 - 2.0 ** (-5.0 - jnp.arange(N_RET_HEADS, dtype=F32)))
    idx = jnp.arange(BLK, dtype=F32)
    dif = idx[:, None] - idx[None, :]
    dmat = jnp.where(dif[None] >= 0, jnp.exp(jnp.maximum(dif, 0.0)[None] * log_gamma[:, None, None]), 0.0)
    zeta = jnp.exp((BLK - 1.0 - idx)[None, :] * log_gamma[:, None])
    xi = jnp.exp((idx + 1.0)[None, :] * log_gamma[:, None])
    decay = jnp.exp(BLK * log_gamma)
    bshape = (N_RET_HEADS, BLK, RET_HEAD_DIM)
    zeta_b = jnp.broadcast_to(zeta[:, :, None], bshape)
    xi_b = jnp.broadcast_to(xi[:, :, None], bshape)
    return cos_t, sin_t, dmat, zeta_b, xi_b, decay


def kernel(x, ffn1_norm, ffn1_w_gate, ffn1_w_up, ffn1_w_down, mix_norm, w_in, attn_sinks, ret_gn_w, w_out,
           ffn2_norm, ffn2_w_gate, ffn2_w_up, ffn2_w_down, final_norm):
    cos_t, sin_t, dmat, zeta_b, xi_b, decay = _tables()
    pre_w = [w[0].astype(BF16) for w in (ffn1_w_gate, ffn1_w_up, ffn1_w_down, w_in)]
    pre_f32 = (ffn1_w_gate, ffn1_w_up, ffn1_w_down, w_in)
    post_f32 = (w_out, ffn2_w_gate, ffn2_w_up, ffn2_w_down)
    def rows8(v, width):
        return jnp.broadcast_to(v.reshape(-1, 1, width), (v.size // width, GAIN_ROWS, width))

    n1, nm, n2 = rows8(ffn1_norm, D_MODEL), rows8(mix_norm, D_MODEL), rows8(ffn2_norm, D_MODEL)
    gnw = rows8(ret_gn_w, RET_WIDTH)
    fn = rows8(final_norm, D_MODEL)[0]

    h = x.reshape(N_TOK, D_MODEL)
    for layer in range(DEPTH):
        final = layer == DEPTH - 1
        wg1, wu1, wd1, win = pre_w
        h, act, wo, wg2, wu2, wd2 = _pre_call(layer, h, n1, wg1, wu1, wd1, nm, win, cos_t, sin_t, post_f32)
        h, *pre_w = _mixpost_call(layer, final, attn_sinks[layer], decay, act, dmat, zeta_b, xi_b, gnw,
                                  h, wo, n2, wg2, wu2, wd2, fn, () if final else pre_f32)
    return h.reshape(BATCH, SEQ, D_MODEL)
```

```python
import functools

import jax
import jax.numpy as jnp
from jax import lax
from jax.experimental import pallas as pl
from jax.experimental.pallas import tpu as pltpu

D_MODEL = 1024
BATCH = 8
SEQ = 2048
DEPTH = 4
ATTN_WIDTH = 512
RET_WIDTH = 512
HEAD_DIM = 64
N_KV_HEADS = 2
GQA_GROUP = 4
KV_WIDTH = 128
WINDOW = 128
BLK = 128
N_RET_HEADS = 4
RET_HEAD_DIM = 128
ROPE_BASE = 10000.0
D_FF = 2816
D_IN = 2816
NORM_EPS = 1e-6
GN_EPS = 1e-5
NEG_INF = -1e30
N_TOK = BATCH * SEQ

W_AQ, W_KV, W_RQ, W_RK, W_RV, W_RG = 0, 512, 768, 1280, 1792, 2304
A_AQ, A_KD, A_VD, A_RQ, A_RK, A_RV, A_SG, D_ACT = 0, 512, 768, 1024, 1536, 2048, 2560, 3072

TM = 512
HALF_TM = TM // 2
N_TILES = N_TOK // TM
TILES_PER_SEQ = SEQ // TM
BLKS_PER_TILE = TM // BLK
FF_CHUNK = 256
OUT_CHUNK = 512
N_PRE_UNITS = 1 + D_FF // FF_CHUNK + D_MODEL // OUT_CHUNK + 6
PRE_SKEW = 3
MIX_STAGES = 3
N_MIX_SLOTS = 4 * BLKS_PER_TILE + MIX_STAGES - 1
N_POST_UNITS = 1 + D_FF // FF_CHUNK + D_MODEL // OUT_CHUNK
POST_SKEW = 2
MIX_DELAY = 1
EARLY_ROUNDS = 1
CAST_ROWS = {D_MODEL: D_MODEL // N_TILES, D_FF: D_FF // (N_TILES // 2)}
VMEM_LIMIT = 56 * 1024 * 1024

F32 = jnp.float32
BF16 = jnp.bfloat16


def _dot(a, b):
    return jnp.dot(a, b, preferred_element_type=F32)


def _dot_nt(a, b):
    return lax.dot_general(a, b, (((1,), (1,)), ((), ())), preferred_element_type=F32)


def _dot_tn(a, b):
    return lax.dot_general(a, b, (((0,), (0,)), ((), ())), preferred_element_type=F32)


def _rms(x, w):
    y = x * lax.rsqrt(jnp.mean(x * x, axis=-1, keepdims=True) + NORM_EPS)
    return y * w


def _swap_halves(x):
    return pltpu.roll(x, x.shape[-1] // 2, 1)


def _cast_blocks(src_refs, dst_refs):
    for src, dst in zip(src_refs, dst_refs, strict=True):
        dst[...] = src[...].astype(BF16)


def _pre_units(rows, x_ref, n1_ref, wg_ref, wu_ref, wd_ref, n2_ref, win_ref, cos_ref, sin_ref,
               h_ref, act_ref, hmid_ref):
    nrow = rows.stop - rows.start
    x = x_ref[rows, :]
    xw = (x * n1_ref[0:1, :]).astype(BF16)
    r = jnp.broadcast_to(lax.rsqrt(jnp.mean(x * x, axis=-1, keepdims=True) + NORM_EPS), (nrow, FF_CHUNK))
    yield
    for c in range(D_FF // FF_CHUNK):
        sl = slice(c * FF_CHUNK, (c + 1) * FF_CHUNK)
        g = _dot(xw, wg_ref[:, sl]) * r
        u = _dot(xw, wu_ref[:, sl]) * r
        hmid_ref[rows, sl] = (jax.nn.silu(g) * u).astype(BF16)
        yield
    for c in range(D_MODEL // OUT_CHUNK):
        sl = slice(c * OUT_CHUNK, (c + 1) * OUT_CHUNK)
        h_ref[rows, sl] = x_ref[rows, sl] + 0.5 * _dot(hmid_ref[rows, :], wd_ref[:, sl])
        yield
    u = _rms(h_ref[rows, :], n2_ref[0:1, :]).astype(BF16)

    def proj(lo, hi):
        return _dot(u, win_ref[:, lo:hi])

    act_ref[rows, A_AQ:A_KD] = (proj(W_AQ, W_KV) * (HEAD_DIM ** -0.5)).astype(BF16)
    yield

    first = lax.broadcasted_iota(jnp.int32, (nrow // 2, KV_WIDTH), 1) < HEAD_DIM
    for part in range(2):
        prow = slice(rows.start + part * (nrow // 2), rows.start + (part + 1) * (nrow // 2))
        kv = _dot(u[part * (nrow // 2):(part + 1) * (nrow // 2), :], win_ref[:, W_KV:W_RQ])
        for src, dst in ((0, A_KD), (KV_WIDTH, A_VD)):
            t = kv[:, src:src + KV_WIDTH]
            ts = _swap_halves(t)
            act_ref[prow, dst:dst + KV_WIDTH] = jnp.where(first, t, ts).astype(BF16)
            act_ref[prow, dst + KV_WIDTH:dst + 2 * KV_WIDTH] = jnp.where(first, ts, t).astype(BF16)
    yield

    pos0 = pl.multiple_of((pl.program_id(0) % TILES_PER_SEQ) * TM + rows.start, nrow)
    cos = cos_ref[pl.ds(pos0, nrow), :]
    sin = sin_ref[pl.ds(pos0, nrow), :]
    for w0, a0, scale in ((W_RQ, A_RQ, 1.0), (W_RK, A_RK, RET_HEAD_DIM ** -0.5)):
        z = proj(w0, w0 + RET_WIDTH)
        for hd in range(N_RET_HEADS):
            zh = z[:, hd * RET_HEAD_DIM:(hd + 1) * RET_HEAD_DIM]
            zr = zh * cos + _swap_halves(zh) * sin
            if scale != 1.0:
                zr = zr * scale
            act_ref[rows, a0 + hd * RET_HEAD_DIM:a0 + (hd + 1) * RET_HEAD_DIM] = zr.astype(BF16)
        yield
    act_ref[rows, A_SG:D_ACT] = jax.nn.silu(proj(W_RG, D_IN)).astype(BF16)
    yield
    act_ref[rows, A_RV:A_SG] = proj(W_RV, W_RG).astype(BF16)
    yield


def _delayed(gen, n):
    for _ in range(n):
        yield
    yield from gen


def _pre_kernel(x_ref, n1_ref, wg_ref, wu_ref, wd_ref, n2_ref, win_ref, cos_ref, sin_ref, *rest):
    cast_src, (h_ref, act_ref), cast_dst, hmid_ref = rest[:4], rest[4:6], rest[6:10], rest[10]
    _cast_blocks(cast_src, cast_dst)
    args = (x_ref, n1_ref, wg_ref, wu_ref, wd_ref, n2_ref, win_ref, cos_ref, sin_ref, h_ref, act_ref, hmid_ref)
    half = TM // 2
    _interleave((_pre_units(slice(0, half), *args), N_PRE_UNITS),
                (_delayed(_pre_units(slice(half, TM), *args), PRE_SKEW), N_PRE_UNITS + PRE_SKEW))


def _mix_units(seq_start, consumed, sink_ref, decay_ref, act_ref, kvp_ref, dmat_ref, zeta_ref, xi_ref, gnw_ref,
               o_ref, state_ref):
    def writable(j):
        return (j * BLK) // (TM // 2) in consumed

    row = lax.broadcasted_iota(jnp.int32, (BLK, 2 * BLK), 0)
    col = lax.broadcasted_iota(jnp.int32, (BLK, 2 * BLK), 1)
    band = (col > row) & (col <= row + WINDOW)
    band0 = band & ((col >= BLK) | jnp.logical_not(seq_start))
    lo = lax.broadcasted_iota(jnp.int32, (BLK, 2 * HEAD_DIM), 1) < HEAD_DIM
    zero_bf = jnp.zeros((BLK, 2 * HEAD_DIM), BF16)
    pair_w = 2 * HEAD_DIM

    def attention(j, kvh):
        r0 = j * BLK
        rows = slice(r0, r0 + BLK)
        valid = band0 if j == 0 else band
        kc = A_KD + kvh * pair_w
        vc = A_VD + kvh * pair_w
        if j == 0:
            k_h = jnp.concatenate([kvp_ref[:, kc - A_KD:kc - A_KD + pair_w], act_ref[0:BLK, kc:kc + pair_w]], axis=0)
            v_h = jnp.concatenate([kvp_ref[:, vc - A_KD:vc - A_KD + pair_w], act_ref[0:BLK, vc:vc + pair_w]], axis=0)
        else:
            k_h = act_ref[r0 - BLK:r0 + BLK, kc:kc + pair_w]
            v_h = act_ref[r0 - BLK:r0 + BLK, vc:vc + pair_w]
        qs = []
        for p in range(GQA_GROUP // 2):
            c0 = A_AQ + (kvh * 2 + p) * pair_w
            qp = act_ref[rows, c0:c0 + pair_w]
            qs.append(jnp.where(lo, qp, zero_bf))
            qs.append(jnp.where(lo, zero_bf, qp))
        s_all = _dot_nt(jnp.concatenate(qs, axis=0), k_h)
        yield
        ps, invs = [], []
        for g in range(GQA_GROUP):
            sink = sink_ref[kvh * GQA_GROUP + g]
            s = jnp.where(valid, s_all[g * BLK:(g + 1) * BLK, :], NEG_INF)
            m = jnp.maximum(jnp.max(s, axis=-1, keepdims=True), sink)
            e = jnp.exp(s - m)
            den = jnp.sum(e, axis=-1, keepdims=True) + jnp.exp(sink - m)
            ps.append(e.astype(BF16))
            invs.append(1.0 / den)
        o_all = _dot(jnp.concatenate(ps, axis=0), v_h)
        yield
        assert writable(j), "mixing output traced before the previous tile's rows were read"
        for p in range(GQA_GROUP // 2):
            c0 = (kvh * 2 + p) * pair_w
            oe = o_all[(2 * p) * BLK:(2 * p + 1) * BLK, :] * invs[2 * p]
            oo = o_all[(2 * p + 1) * BLK:(2 * p + 2) * BLK, :] * invs[2 * p + 1]
            o_ref[rows, c0:c0 + pair_w] = jnp.where(lo, oe, oo).astype(BF16)

    def block_diag(a, b):
        zero = jnp.zeros_like(a)
        return jnp.concatenate([jnp.concatenate([a, zero], axis=1), jnp.concatenate([zero, b], axis=1)], axis=0)

    def retention(j, heads):
        rows = slice(j * BLK, (j + 1) * BLK)
        d = RET_HEAD_DIM
        c0 = heads[0] * d
        q2 = act_ref[rows, A_RQ + c0:A_RQ + c0 + 2 * d]
        k2 = act_ref[rows, A_RK + c0:A_RK + c0 + 2 * d]
        v2 = act_ref[rows, A_RV + c0:A_RV + c0 + 2 * d]
        sc2 = _dot_nt(q2, block_diag(k2[:, :d], k2[:, d:]))
        cross2 = _dot(q2, block_diag(state_ref[heads[0]].astype(BF16), state_ref[heads[1]].astype(BF16)))
        kvs = []
        for i, hd in enumerate(heads):
            vz = (v2[:, i * d:(i + 1) * d].astype(F32) * zeta_ref[hd]).astype(BF16)
            kvs.append(_dot_tn(k2[:, i * d:(i + 1) * d], vz))
        yield
        for hd, kv in zip(heads, kvs, strict=True):
            state_ref[hd] = state_ref[hd] * decay_ref[hd] + kv
        sc = jnp.concatenate([sc2[:, i * d:(i + 1) * d] * dmat_ref[hd] for i, hd in enumerate(heads)], axis=1)
        intra2 = _dot(sc.astype(BF16), block_diag(v2[:, :d], v2[:, d:]))
        yield
        assert writable(j), "mixing output traced before the previous tile's rows were read"
        for i, hd in enumerate(heads):
            c = hd * RET_HEAD_DIM
            y = intra2[:, i * d:(i + 1) * d] + cross2[:, i * d:(i + 1) * d] * xi_ref[hd]
            mu = jnp.mean(y, axis=-1, keepdims=True)
            dev = y - mu
            var = jnp.mean(dev * dev, axis=-1, keepdims=True)
            yn = dev * lax.rsqrt(var + GN_EPS) * gnw_ref[0:1, c:c + RET_HEAD_DIM]
            sg = act_ref[rows, A_SG + c:A_SG + c + RET_HEAD_DIM]
            o_ref[rows, ATTN_WIDTH + c:ATTN_WIDTH + c + RET_HEAD_DIM] = (sg.astype(F32) * yn).astype(BF16)

    chains = []
    for j in range(BLKS_PER_TILE):
        chains += [attention(j, 0), attention(j, 1), retention(j, (0, 1)), retention(j, (2, 3))]
    active = []
    for chain in chains + [None] * (MIX_STAGES - 1):
        if chain is not None:
            active.append(chain)
        for g in list(active):
            try:
                next(g)
            except StopIteration:
                active.remove(g)
        yield


def _post_units(final, half, consumed, ar_ref, h_ref, wo_ref, n_ref, wg_ref, wu_ref, wd_ref, fn_ref, o_ref,
                hmid_ref):
    if isinstance(half, int):
        rows = slice(half * HALF_TM, (half + 1) * HALF_TM)
    else:
        rows = pl.ds(pl.multiple_of(half * HALF_TM, HALF_TM), HALF_TM)
    h2 = h_ref[rows, :] + _dot(ar_ref[rows, :], wo_ref[...])
    if isinstance(half, int):
        consumed.add(half)
    o_ref[rows, :] = h2
    xn = _rms(h2, n_ref[0:1, :]).astype(BF16)
    yield
    for c in range(D_FF // FF_CHUNK):
        sl = slice(c * FF_CHUNK, (c + 1) * FF_CHUNK)
        g = _dot(xn, wg_ref[:, sl])
        u = _dot(xn, wu_ref[:, sl])
        hmid_ref[rows, sl] = (jax.nn.silu(g) * u).astype(BF16)
        yield
    for c in range(D_MODEL // OUT_CHUNK):
        sl = slice(c * OUT_CHUNK, (c + 1) * OUT_CHUNK)
        o_ref[rows, sl] = o_ref[rows, sl] + 0.5 * _dot(hmid_ref[rows, :], wd_ref[:, sl])
        yield
    if final:
        o_ref[rows, :] = _rms(o_ref[rows, :], fn_ref[0:1, :])


def _interleave(*streams, early=()):
    total = max(n for _, n in streams)
    done = [0] * len(streams)
    for k in range(1, total + 1):
        for idx, (gen, n) in enumerate(streams):
            span = total - EARLY_ROUNDS if idx in early else total
            while done[idx] < -(-min(k, span) * n // span):
                next(gen)
                done[idx] += 1
    for gen, _ in streams:
        for _ in gen:
            raise AssertionError("stream yielded more often than declared")


def _mixpost_kernel(final, sink_ref, decay_ref, act_ref, kvp_ref, dmat_ref, zeta_ref, xi_ref, gnw_ref,
                    h_ref, wo_ref, n_ref, wg_ref, wu_ref, wd_ref, fn_ref, *rest):
    n_cast = 0 if final else 4
    cast_src, o_ref, cast_dst = rest[:n_cast], rest[n_cast], rest[n_cast + 1:2 * n_cast + 1]
    state_ref, ar_ref, hmid_ref = rest[2 * n_cast + 1:]
    t = pl.program_id(0)
    seq_start = (jnp.minimum(t, N_TILES - 1) % TILES_PER_SEQ) == 0

    @pl.when(seq_start)
    def _():
        state_ref[...] = jnp.zeros_like(state_ref)

    post_refs = (ar_ref, h_ref, wo_ref, n_ref, wg_ref, wu_ref, wd_ref, fn_ref, o_ref, hmid_ref)

    def post_streams(consumed):
        return ((_post_units(final, 0, consumed, *post_refs), N_POST_UNITS),
                (_delayed(_post_units(final, 1, consumed, *post_refs), POST_SKEW), N_POST_UNITS + POST_SKEW))

    def mix_stream(consumed):
        return (_delayed(_mix_units(seq_start, consumed, sink_ref, decay_ref, act_ref, kvp_ref, dmat_ref, zeta_ref,
                                    xi_ref, gnw_ref, ar_ref, state_ref), MIX_DELAY), N_MIX_SLOTS + MIX_DELAY)

    @pl.when(t == 0)
    def _():
        _cast_blocks(cast_src, cast_dst)
        _interleave(mix_stream({0, 1}))

    @pl.when((t > 0) & (t < N_TILES))
    def _():
        _cast_blocks(cast_src, cast_dst)
        consumed = set()
        _interleave(*post_streams(consumed), mix_stream(consumed), early=(2,))

    @pl.when(t == N_TILES)
    def _():
        _cast_blocks(cast_src, cast_dst)

        def one_half(half, carry):
            for _ in _post_units(final, half, None, *post_refs):
                pass
            return carry

        lax.fori_loop(0, TM // HALF_TM, one_half, 0)


def _resident(shape, layer):
    nd = len(shape)
    return pl.BlockSpec((None,) + shape, lambda *_: (layer,) + (0,) * nd, pipeline_mode=pl.Buffered(1))


def _const(shape):
    nd = len(shape)
    return pl.BlockSpec(shape, lambda *_: (0,) * nd, pipeline_mode=pl.Buffered(1))


def _cast_specs(layer, shapes):
    ins, outs, shapes_out = [], [], []
    for rows, cols in shapes:
        blk = CAST_ROWS[rows]
        last = rows // blk - 1
        ins.append(pl.BlockSpec((None, blk, cols), lambda i, last=last: (layer, jnp.minimum(i, last), 0)))
        outs.append(pl.BlockSpec((blk, cols), lambda i, last=last: (jnp.minimum(i, last), 0)))
        shapes_out.append(jax.ShapeDtypeStruct((rows, cols), BF16))
    return ins, outs, shapes_out


def _pre_call(layer, x, n1, wg, wu, wd, n2, win, cos_t, sin_t, next_f32):
    tok = lambda w: pl.BlockSpec((TM, w), lambda i: (i, 0))
    cast_in, cast_out, cast_shapes = _cast_specs(
        layer, ((D_MODEL, D_MODEL), (D_MODEL, D_FF), (D_MODEL, D_FF), (D_FF, D_MODEL)))
    return pl.pallas_call(
        _pre_kernel,
        grid=(N_TILES,),
        in_specs=[
            tok(D_MODEL),
            _resident((1, D_MODEL), layer),
            _const((D_MODEL, D_FF)),
            _const((D_MODEL, D_FF)),
            _const((D_FF, D_MODEL)),
            _resident((1, D_MODEL), layer),
            _const((D_MODEL, D_IN)),
            _const((SEQ, RET_HEAD_DIM)),
            _const((SEQ, RET_HEAD_DIM)),
        ] + cast_in,
        out_specs=[tok(D_MODEL), tok(D_ACT)] + cast_out,
        out_shape=[jax.ShapeDtypeStruct((N_TOK, D_MODEL), F32),
                   jax.ShapeDtypeStruct((N_TOK, D_ACT), BF16)] + cast_shapes,
        scratch_shapes=[pltpu.VMEM((TM, D_FF), BF16)],
        compiler_params=pltpu.CompilerParams(dimension_semantics=("arbitrary",), vmem_limit_bytes=VMEM_LIMIT),
        name=f"pre{layer}",
    )(x, n1, wg, wu, wd, n2, win, cos_t, sin_t, *next_f32)


def _mixpost_call(layer, final, sinks, decay, act, dmat, zeta_b, xi_b, gnw, h, wo, n, wg, wu, wd, fn, next_f32):
    def mix_tile(i):
        return jnp.minimum(i, N_TILES - 1)

    if final:
        cast_in, cast_out, cast_shapes = [], [], []
    else:
        cast_in, cast_out, cast_shapes = _cast_specs(
            layer + 1, ((D_MODEL, D_FF), (D_MODEL, D_FF), (D_FF, D_MODEL), (D_MODEL, D_IN)))

    prev_kv = pl.BlockSpec((BLK, A_RQ - A_KD),
                           lambda i: (jnp.maximum(mix_tile(i) * BLKS_PER_TILE - 1, 0), A_KD // (A_RQ - A_KD)))
    lag = pl.BlockSpec((TM, D_MODEL), lambda i: (jnp.maximum(i - 1, 0), 0))
    smem = pl.BlockSpec(memory_space=pltpu.SMEM)
    return pl.pallas_call(
        functools.partial(_mixpost_kernel, final),
        grid=(N_TILES + 1,),
        in_specs=[smem, smem,
                  pl.BlockSpec((TM, D_ACT), lambda i: (mix_tile(i), 0)),
                  prev_kv,
                  _const((N_RET_HEADS, BLK, BLK)), _const((N_RET_HEADS, BLK, BLK)), _const((N_RET_HEADS, BLK, BLK)),
                  _resident((1, RET_WIDTH), layer),
                  lag,
                  _const((D_MODEL, D_MODEL)),
                  _resident((1, D_MODEL), layer),
                  _const((D_MODEL, D_FF)),
                  _const((D_MODEL, D_FF)),
                  _const((D_FF, D_MODEL)),
                  _const((1, D_MODEL))] + cast_in,
        out_specs=[lag] + cast_out,
        out_shape=[jax.ShapeDtypeStruct((N_TOK, D_MODEL), F32)] + cast_shapes,
        scratch_shapes=[pltpu.VMEM((N_RET_HEADS, RET_HEAD_DIM, RET_HEAD_DIM), F32),
                        pltpu.VMEM((TM, D_MODEL), BF16),
                        pltpu.VMEM((TM, D_FF), BF16)],
        compiler_params=pltpu.CompilerParams(dimension_semantics=("arbitrary",), vmem_limit_bytes=VMEM_LIMIT),
        name=f"mixpost{layer}",
    )(sinks, decay, act, act, dmat, zeta_b, xi_b, gnw, h, wo, n, wg, wu, wd, fn, *next_f32)


def _tables():
    pos = jnp.arange(SEQ, dtype=F32)
    inv_freq = ROPE_BASE ** (-jnp.arange(0, RET_HEAD_DIM, 2, dtype=F32) / RET_HEAD_DIM)
    ang = pos[:, None] * inv_freq[None, :]
    cos, sin = jnp.cos(ang), jnp.sin(ang)
    cos_t = jnp.concatenate([cos, cos], axis=-1)
    sin_t = jnp.concatenate([-sin, sin], axis=-1)
    log_gamma = jnp.log(1.0 - 2.0 ** (-5.0 - jnp.arange(N_RET_HEADS, dtype=F32)))
    idx = jnp.arange(BLK, dtype=F32)
    dif = idx[:, None] - idx[None, :]
    dmat = jnp.where(dif[None] >= 0, jnp.exp(jnp.maximum(dif, 0.0)[None] * log_gamma[:, None, None]), 0.0)
    zeta = jnp.exp((BLK - 1.0 - idx)[None, :] * log_gamma[:, None])
    xi = jnp.exp((idx + 1.0)[None, :] * log_gamma[:, None])
    decay = jnp.exp(BLK * log_gamma)
    bshape = (N_RET_HEADS, BLK, RET_HEAD_DIM)
    zeta_b = jnp.broadcast_to(zeta[:, :, None], bshape)
    xi_b = jnp.broadcast_to(xi[:, :, None], bshape)
    return cos_t, sin_t, dmat, zeta_b, xi_b, decay


def kernel(x, ffn1_norm, ffn1_w_gate, ffn1_w_up, ffn1_w_down, mix_norm, w_in, attn_sinks, ret_gn_w, w_out,
           ffn2_norm, ffn2_w_gate, ffn2_w_up, ffn2_w_down, final_norm):
    cos_t, sin_t, dmat, zeta_b, xi_b, decay = _tables()
    pre_w = [w[0].astype(BF16) for w in (ffn1_w_gate, ffn1_w_up, ffn1_w_down, w_in)]
    pre_f32 = (ffn1_w_gate, ffn1_w_up, ffn1_w_down, w_in)
    post_f32 = (w_out, ffn2_w_gate, ffn2_w_up, ffn2_w_down)
    n1 = ffn1_norm.reshape(DEPTH, 1, D_MODEL)
    nm = mix_norm.reshape(DEPTH, 1, D_MODEL)
    n2 = ffn2_norm.reshape(DEPTH, 1, D_MODEL)
    gnw = ret_gn_w.reshape(DEPTH, 1, RET_WIDTH)
    fn = final_norm.reshape(1, D_MODEL)

    h = x.reshape(N_TOK, D_MODEL)
    for layer in range(DEPTH):
        final = layer == DEPTH - 1
        wg1, wu1, wd1, win = pre_w
        h, act, wo, wg2, wu2, wd2 = _pre_call(layer, h, n1, wg1, wu1, wd1, nm, win, cos_t, sin_t, post_f32)
        h, *pre_w = _mixpost_call(layer, final, attn_sinks[layer], decay, act, dmat, zeta_b, xi_b, gnw,
                                  h, wo, n2, wg2, wu2, wd2, fn, () if final else pre_f32)
    return h.reshape(BATCH, SEQ, D_MODEL)
```

```python
import functools

import jax
import jax.numpy as jnp
from jax import lax
from jax.experimental import pallas as pl
from jax.experimental.pallas import tpu as pltpu

D_MODEL = 1024
BATCH = 8
SEQ = 2048
DEPTH = 4
ATTN_WIDTH = 512
RET_WIDTH = 512
HEAD_DIM = 64
N_KV_HEADS = 2
GQA_GROUP = 4
KV_WIDTH = 128
WINDOW = 128
BLK = 128
N_RET_HEADS = 4
RET_HEAD_DIM = 128
ROPE_BASE = 10000.0
D_FF = 2816
D_IN = 2816
NORM_EPS = 1e-6
GN_EPS = 1e-5
NEG_INF = -1e30
N_TOK = BATCH * SEQ

W_AQ, W_KV, W_RQ, W_RK, W_RV, W_RG = 0, 512, 768, 1280, 1792, 2304
A_AQ, A_KD, A_VD, A_RQ, A_RK, A_RV, A_SG, A_END = 0, 512, 768, 1024, 1536, 2048, 2560, 3072
D_ACT = A_END + 128

TM = 512
HALF_TM = TM // 2
N_TILES = N_TOK // TM
TILES_PER_SEQ = SEQ // TM
BLKS_PER_TILE = TM // BLK
FF_CHUNK = 256
OUT_CHUNK = 512
N_PRE_UNITS = 1 + D_FF // FF_CHUNK + D_MODEL // OUT_CHUNK + 6
PRE_SKEW = 3
MIX_STAGES = 3
N_MIX_SLOTS = 4 * BLKS_PER_TILE + MIX_STAGES - 1
N_POST_UNITS = 1 + D_FF // FF_CHUNK + D_MODEL // OUT_CHUNK
POST_SKEW = 2
MIX_DELAY = 1
EARLY_ROUNDS = 1
CAST_ROWS = {D_MODEL: D_MODEL // N_TILES, D_FF: D_FF // (N_TILES // 2)}
VMEM_LIMIT = 56 * 1024 * 1024

F32 = jnp.float32
BF16 = jnp.bfloat16


def _dot(a, b):
    return jnp.dot(a, b, preferred_element_type=F32)


def _dot_nt(a, b):
    return lax.dot_general(a, b, (((1,), (1,)), ((), ())), preferred_element_type=F32)


def _dot_tn(a, b):
    return lax.dot_general(a, b, (((0,), (0,)), ((), ())), preferred_element_type=F32)


def _rms(x, w):
    y = x * lax.rsqrt(jnp.mean(x * x, axis=-1, keepdims=True) + NORM_EPS)
    return y * w


def _swap_halves(x):
    return pltpu.roll(x, x.shape[-1] // 2, 1)


def _cast_blocks(src_refs, dst_refs):
    for src, dst in zip(src_refs, dst_refs, strict=True):
        dst[...] = src[...].astype(BF16)


def _pre_units(rows, x_ref, n1_ref, wg_ref, wu_ref, wd_ref, n2_ref, win_ref, cos_ref, sin_ref,
               h_ref, act_ref, hmid_ref):
    nrow = rows.stop - rows.start
    x = x_ref[rows, :]
    xw = (x * n1_ref[0:1, :]).astype(BF16)
    r = jnp.broadcast_to(lax.rsqrt(jnp.mean(x * x, axis=-1, keepdims=True) + NORM_EPS), (nrow, FF_CHUNK))
    yield
    for c in range(D_FF // FF_CHUNK):
        sl = slice(c * FF_CHUNK, (c + 1) * FF_CHUNK)
        g = _dot(xw, wg_ref[:, sl]) * r
        u = _dot(xw, wu_ref[:, sl]) * r
        hmid_ref[rows, sl] = (jax.nn.silu(g) * u).astype(BF16)
        yield
    for c in range(D_MODEL // OUT_CHUNK):
        sl = slice(c * OUT_CHUNK, (c + 1) * OUT_CHUNK)
        h_ref[rows, sl] = x_ref[rows, sl] + 0.5 * _dot(hmid_ref[rows, :], wd_ref[:, sl])
        yield
    u = _rms(h_ref[rows, :], n2_ref[0:1, :]).astype(BF16)

    def proj(lo, hi):
        return _dot(u, win_ref[:, lo:hi])

    act_ref[rows, A_AQ:A_KD] = (proj(W_AQ, W_KV) * (HEAD_DIM ** -0.5)).astype(BF16)
    yield

    first = lax.broadcasted_iota(jnp.int32, (nrow // 2, KV_WIDTH), 1) < HEAD_DIM
    for part in range(2):
        prow = slice(rows.start + part * (nrow // 2), rows.start + (part + 1) * (nrow // 2))
        kv = _dot(u[part * (nrow // 2):(part + 1) * (nrow // 2), :], win_ref[:, W_KV:W_RQ])
        for src, dst in ((0, A_KD), (KV_WIDTH, A_VD)):
            t = kv[:, src:src + KV_WIDTH]
            ts = _swap_halves(t)
            act_ref[prow, dst:dst + KV_WIDTH] = jnp.where(first, t, ts).astype(BF16)
            act_ref[prow, dst + KV_WIDTH:dst + 2 * KV_WIDTH] = jnp.where(first, ts, t).astype(BF16)
    yield

    pos0 = pl.multiple_of((pl.program_id(0) % TILES_PER_SEQ) * TM + rows.start, nrow)
    cos = cos_ref[pl.ds(pos0, nrow), :]
    sin = sin_ref[pl.ds(pos0, nrow), :]
    for w0, a0, scale in ((W_RQ, A_RQ, 1.0), (W_RK, A_RK, RET_HEAD_DIM ** -0.5)):
        z = proj(w0, w0 + RET_WIDTH)
        for hd in range(N_RET_HEADS):
            zh = z[:, hd * RET_HEAD_DIM:(hd + 1) * RET_HEAD_DIM]
            zr = zh * cos + _swap_halves(zh) * sin
            if scale != 1.0:
                zr = zr * scale
            act_ref[rows, a0 + hd * RET_HEAD_DIM:a0 + (hd + 1) * RET_HEAD_DIM] = zr.astype(BF16)
        yield
    act_ref[rows, A_SG:A_END] = jax.nn.silu(proj(W_RG, D_IN)).astype(BF16)
    act_ref[rows, A_END:D_ACT] = jnp.zeros((nrow, D_ACT - A_END), BF16)
    yield
    act_ref[rows, A_RV:A_SG] = proj(W_RV, W_RG).astype(BF16)
    yield


def _delayed(gen, n):
    for _ in range(n):
        yield
    yield from gen


def _pre_kernel(x_ref, n1_ref, wg_ref, wu_ref, wd_ref, n2_ref, win_ref, cos_ref, sin_ref, *rest):
    cast_src, (h_ref, act_ref), cast_dst, hmid_ref = rest[:4], rest[4:6], rest[6:10], rest[10]
    _cast_blocks(cast_src, cast_dst)
    args = (x_ref, n1_ref, wg_ref, wu_ref, wd_ref, n2_ref, win_ref, cos_ref, sin_ref, h_ref, act_ref, hmid_ref)
    half = TM // 2
    _interleave((_pre_units(slice(0, half), *args), N_PRE_UNITS),
                (_delayed(_pre_units(slice(half, TM), *args), PRE_SKEW), N_PRE_UNITS + PRE_SKEW))


def _mix_units(seq_start, consumed, sink_ref, decay_ref, act_ref, kvp_ref, dmat_ref, zeta_ref, xi_ref, gnw_ref,
               o_ref, state_ref):
    def writable(j):
        return (j * BLK) // (TM // 2) in consumed

    row = lax.broadcasted_iota(jnp.int32, (BLK, 2 * BLK), 0)
    col = lax.broadcasted_iota(jnp.int32, (BLK, 2 * BLK), 1)
    band = (col > row) & (col <= row + WINDOW)
    band0 = band & ((col >= BLK) | jnp.logical_not(seq_start))
    lo = lax.broadcasted_iota(jnp.int32, (BLK, 2 * HEAD_DIM), 1) < HEAD_DIM
    zero_bf = jnp.zeros((BLK, 2 * HEAD_DIM), BF16)
    pair_w = 2 * HEAD_DIM

    def attention(j, kvh):
        r0 = j * BLK
        rows = slice(r0, r0 + BLK)
        valid = band0 if j == 0 else band
        kc = A_KD + kvh * pair_w
        vc = A_VD + kvh * pair_w
        if j == 0:
            k_h = jnp.concatenate([kvp_ref[:, kc - A_KD:kc - A_KD + pair_w], act_ref[0:BLK, kc:kc + pair_w]], axis=0)
            v_h = jnp.concatenate([kvp_ref[:, vc - A_KD:vc - A_KD + pair_w], act_ref[0:BLK, vc:vc + pair_w]], axis=0)
        else:
            k_h = act_ref[r0 - BLK:r0 + BLK, kc:kc + pair_w]
            v_h = act_ref[r0 - BLK:r0 + BLK, vc:vc + pair_w]
        qs = []
        for p in range(GQA_GROUP // 2):
            c0 = A_AQ + (kvh * 2 + p) * pair_w
            qp = act_ref[rows, c0:c0 + pair_w]
            qs.append(jnp.where(lo, qp, zero_bf))
            qs.append(jnp.where(lo, zero_bf, qp))
        s_all = _dot_nt(jnp.concatenate(qs, axis=0), k_h)
        yield
        ps, invs = [], []
        for g in range(GQA_GROUP):
            sink = sink_ref[kvh * GQA_GROUP + g]
            s = jnp.where(valid, s_all[g * BLK:(g + 1) * BLK, :], NEG_INF)
            m = jnp.maximum(jnp.max(s, axis=-1, keepdims=True), sink)
            e = jnp.exp(s - m)
            den = jnp.sum(e, axis=-1, keepdims=True) + jnp.exp(sink - m)
            ps.append(e.astype(BF16))
            invs.append(1.0 / den)
        o_all = _dot(jnp.concatenate(ps, axis=0), v_h)
        yield
        assert writable(j), "mixing output traced before the previous tile's rows were read"
        for p in range(GQA_GROUP // 2):
            c0 = (kvh * 2 + p) * pair_w
            oe = o_all[(2 * p) * BLK:(2 * p + 1) * BLK, :] * invs[2 * p]
            oo = o_all[(2 * p + 1) * BLK:(2 * p + 2) * BLK, :] * invs[2 * p + 1]
            o_ref[rows, c0:c0 + pair_w] = jnp.where(lo, oe, oo).astype(BF16)

    def block_diag(a, b):
        zero = jnp.zeros_like(a)
        return jnp.concatenate([jnp.concatenate([a, zero], axis=1), jnp.concatenate([zero, b], axis=1)], axis=0)

    def retention(j, heads):
        rows = slice(j * BLK, (j + 1) * BLK)
        d = RET_HEAD_DIM
        c0 = heads[0] * d
        q2 = act_ref[rows, A_RQ + c0:A_RQ + c0 + 2 * d]
        k2 = act_ref[rows, A_RK + c0:A_RK + c0 + 2 * d]
        v2 = act_ref[rows, A_RV + c0:A_RV + c0 + 2 * d]
        sc2 = _dot_nt(q2, block_diag(k2[:, :d], k2[:, d:]))
        cross2 = _dot(q2, block_diag(state_ref[heads[0]].astype(BF16), state_ref[heads[1]].astype(BF16)))
        kvs = []
        for i, hd in enumerate(heads):
            vz = (v2[:, i * d:(i + 1) * d].astype(F32) * zeta_ref[hd]).astype(BF16)
            kvs.append(_dot_tn(k2[:, i * d:(i + 1) * d], vz))
        yield
        for hd, kv in zip(heads, kvs, strict=True):
            state_ref[hd] = state_ref[hd] * decay_ref[hd] + kv
        sc = jnp.concatenate([sc2[:, i * d:(i + 1) * d] * dmat_ref[hd] for i, hd in enumerate(heads)], axis=1)
        intra2 = _dot(sc.astype(BF16), block_diag(v2[:, :d], v2[:, d:]))
        yield
        assert writable(j), "mixing output traced before the previous tile's rows were read"
        for i, hd in enumerate(heads):
            c = hd * RET_HEAD_DIM
            y = intra2[:, i * d:(i + 1) * d] + cross2[:, i * d:(i + 1) * d] * xi_ref[hd]
            mu = jnp.mean(y, axis=-1, keepdims=True)
            dev = y - mu
            var = jnp.mean(dev * dev, axis=-1, keepdims=True)
            yn = dev * lax.rsqrt(var + GN_EPS) * gnw_ref[0:1, c:c + RET_HEAD_DIM]
            sg = act_ref[rows, A_SG + c:A_SG + c + RET_HEAD_DIM]
            o_ref[rows, ATTN_WIDTH + c:ATTN_WIDTH + c + RET_HEAD_DIM] = (sg.astype(F32) * yn).astype(BF16)

    chains = []
    for j in range(BLKS_PER_TILE):
        chains += [attention(j, 0), attention(j, 1), retention(j, (0, 1)), retention(j, (2, 3))]
    active = []
    for chain in chains + [None] * (MIX_STAGES - 1):
        if chain is not None:
            active.append(chain)
        for g in list(active):
            try:
                next(g)
            except StopIteration:
                active.remove(g)
        yield


def _post_units(final, half, consumed, ar_ref, h_ref, wo_ref, n_ref, wg_ref, wu_ref, wd_ref, fn_ref, o_ref,
                hmid_ref):
    if isinstance(half, int):
        rows = slice(half * HALF_TM, (half + 1) * HALF_TM)
    else:
        rows = pl.ds(pl.multiple_of(half * HALF_TM, HALF_TM), HALF_TM)
    h2 = h_ref[rows, :] + _dot(ar_ref[rows, :], wo_ref[...])
    if isinstance(half, int):
        consumed.add(half)
    o_ref[rows, :] = h2
    xn = _rms(h2, n_ref[0:1, :]).astype(BF16)
    yield
    for c in range(D_FF // FF_CHUNK):
        sl = slice(c * FF_CHUNK, (c + 1) * FF_CHUNK)
        g = _dot(xn, wg_ref[:, sl])
        u = _dot(xn, wu_ref[:, sl])
        hmid_ref[rows, sl] = (jax.nn.silu(g) * u).astype(BF16)
        yield
    for c in range(D_MODEL // OUT_CHUNK):
        sl = slice(c * OUT_CHUNK, (c + 1) * OUT_CHUNK)
        o_ref[rows, sl] = o_ref[rows, sl] + 0.5 * _dot(hmid_ref[rows, :], wd_ref[:, sl])
        yield
    if final:
        o_ref[rows, :] = _rms(o_ref[rows, :], fn_ref[0:1, :])


def _interleave(*streams, early=()):
    total = max(n for _, n in streams)
    done = [0] * len(streams)
    for k in range(1, total + 1):
        for idx, (gen, n) in enumerate(streams):
            span = total - EARLY_ROUNDS if idx in early else total
            while done[idx] < -(-min(k, span) * n // span):
                next(gen)
                done[idx] += 1
    for gen, _ in streams:
        for _ in gen:
            raise AssertionError("stream yielded more often than declared")


def _mixpost_kernel(final, sink_ref, decay_ref, act_ref, kvp_ref, dmat_ref, zeta_ref, xi_ref, gnw_ref,
                    h_ref, wo_ref, n_ref, wg_ref, wu_ref, wd_ref, fn_ref, *rest):
    n_cast = 0 if final else 4
    cast_src, o_ref, cast_dst = rest[:n_cast], rest[n_cast], rest[n_cast + 1:2 * n_cast + 1]
    state_ref, ar_ref, hmid_ref = rest[2 * n_cast + 1:]
    t = pl.program_id(0)
    seq_start = (jnp.minimum(t, N_TILES - 1) % TILES_PER_SEQ) == 0

    @pl.when(seq_start)
    def _():
        state_ref[...] = jnp.zeros_like(state_ref)

    post_refs = (ar_ref, h_ref, wo_ref, n_ref, wg_ref, wu_ref, wd_ref, fn_ref, o_ref, hmid_ref)

    def post_streams(consumed):
        return ((_post_units(final, 0, consumed, *post_refs), N_POST_UNITS),
                (_delayed(_post_units(final, 1, consumed, *post_refs), POST_SKEW), N_POST_UNITS + POST_SKEW))

    def mix_stream(consumed):
        return (_delayed(_mix_units(seq_start, consumed, sink_ref, decay_ref, act_ref, kvp_ref, dmat_ref, zeta_ref,
                                    xi_ref, gnw_ref, ar_ref, state_ref), MIX_DELAY), N_MIX_SLOTS + MIX_DELAY)

    @pl.when(t == 0)
    def _():
        _cast_blocks(cast_src, cast_dst)
        _interleave(mix_stream({0, 1}))

    @pl.when((t > 0) & (t < N_TILES))
    def _():
        _cast_blocks(cast_src, cast_dst)
        consumed = set()
        _interleave(*post_streams(consumed), mix_stream(consumed), early=(2,))

    @pl.when(t == N_TILES)
    def _():
        _cast_blocks(cast_src, cast_dst)

        def one_half(half, carry):
            for _ in _post_units(final, half, None, *post_refs):
                pass
            return carry

        lax.fori_loop(0, TM // HALF_TM, one_half, 0)


def _resident(shape, layer):
    nd = len(shape)
    return pl.BlockSpec((None,) + shape, lambda *_: (layer,) + (0,) * nd, pipeline_mode=pl.Buffered(1))


def _const(shape):
    nd = len(shape)
    return pl.BlockSpec(shape, lambda *_: (0,) * nd, pipeline_mode=pl.Buffered(1))


def _cast_specs(layer, shapes):
    ins, outs, shapes_out = [], [], []
    for rows, cols in shapes:
        blk = CAST_ROWS[rows]
        last = rows // blk - 1
        ins.append(pl.BlockSpec((None, blk, cols), lambda i, last=last: (layer, jnp.minimum(i, last), 0)))
        outs.append(pl.BlockSpec((blk, cols), lambda i, last=last: (jnp.minimum(i, last), 0)))
        shapes_out.append(jax.ShapeDtypeStruct((rows, cols), BF16))
    return ins, outs, shapes_out


def _pre_call(layer, x, n1, wg, wu, wd, n2, win, cos_t, sin_t, next_f32):
    tok = lambda w: pl.BlockSpec((TM, w), lambda i: (i, 0))
    cast_in, cast_out, cast_shapes = _cast_specs(
        layer, ((D_MODEL, D_MODEL), (D_MODEL, D_FF), (D_MODEL, D_FF), (D_FF, D_MODEL)))
    return pl.pallas_call(
        _pre_kernel,
        grid=(N_TILES,),
        in_specs=[
            tok(D_MODEL),
            _resident((1, D_MODEL), layer),
            _const((D_MODEL, D_FF)),
            _const((D_MODEL, D_FF)),
            _const((D_FF, D_MODEL)),
            _resident((1, D_MODEL), layer),
            _const((D_MODEL, D_IN)),
            _const((SEQ, RET_HEAD_DIM)),
            _const((SEQ, RET_HEAD_DIM)),
        ] + cast_in,
        out_specs=[tok(D_MODEL), tok(D_ACT)] + cast_out,
        out_shape=[jax.ShapeDtypeStruct((N_TOK, D_MODEL), F32),
                   jax.ShapeDtypeStruct((N_TOK, D_ACT), BF16)] + cast_shapes,
        scratch_shapes=[pltpu.VMEM((TM, D_FF), BF16)],
        compiler_params=pltpu.CompilerParams(dimension_semantics=("arbitrary",), vmem_limit_bytes=VMEM_LIMIT),
        name=f"pre{layer}",
    )(x, n1, wg, wu, wd, n2, win, cos_t, sin_t, *next_f32)


def _mixpost_call(layer, final, sinks, decay, act, dmat, zeta_b, xi_b, gnw, h, wo, n, wg, wu, wd, fn, next_f32):
    def mix_tile(i):
        return jnp.minimum(i, N_TILES - 1)

    if final:
        cast_in, cast_out, cast_shapes = [], [], []
    else:
        cast_in, cast_out, cast_shapes = _cast_specs(
            layer + 1, ((D_MODEL, D_FF), (D_MODEL, D_FF), (D_FF, D_MODEL), (D_MODEL, D_IN)))

    prev_kv = pl.BlockSpec((BLK, A_RQ - A_KD),
                           lambda i: (jnp.maximum(mix_tile(i) * BLKS_PER_TILE - 1, 0), A_KD // (A_RQ - A_KD)))
    lag = pl.BlockSpec((TM, D_MODEL), lambda i: (jnp.maximum(i - 1, 0), 0))
    smem = pl.BlockSpec(memory_space=pltpu.SMEM)
    return pl.pallas_call(
        functools.partial(_mixpost_kernel, final),
        grid=(N_TILES + 1,),
        in_specs=[smem, smem,
                  pl.BlockSpec((TM, D_ACT), lambda i: (mix_tile(i), 0)),
                  prev_kv,
                  _const((N_RET_HEADS, BLK, BLK)), _const((N_RET_HEADS, BLK, BLK)), _const((N_RET_HEADS, BLK, BLK)),
                  _resident((1, RET_WIDTH), layer),
                  lag,
                  _const((D_MODEL, D_MODEL)),
                  _resident((1, D_MODEL), layer),
                  _const((D_MODEL, D_FF)),
                  _const((D_MODEL, D_FF)),
                  _const((D_FF, D_MODEL)),
                  _const((1, D_MODEL))] + cast_in,
        out_specs=[lag] + cast_out,
        out_shape=[jax.ShapeDtypeStruct((N_TOK, D_MODEL), F32)] + cast_shapes,
        scratch_shapes=[pltpu.VMEM((N_RET_HEADS, RET_HEAD_DIM, RET_HEAD_DIM), F32),
                        pltpu.VMEM((TM, D_MODEL), BF16),
                        pltpu.VMEM((TM, D_FF), BF16)],
        compiler_params=pltpu.CompilerParams(dimension_semantics=("arbitrary",), vmem_limit_bytes=VMEM_LIMIT),
        name=f"mixpost{layer}",
    )(sinks, decay, act, act, dmat, zeta_b, xi_b, gnw, h, wo, n, wg, wu, wd, fn, *next_f32)


def _tables():
    pos = jnp.arange(SEQ, dtype=F32)
    inv_freq = ROPE_BASE ** (-jnp.arange(0, RET_HEAD_DIM, 2, dtype=F32) / RET_HEAD_DIM)
    ang = pos[:, None] * inv_freq[None, :]
    cos, sin = jnp.cos(ang), jnp.sin(ang)
    cos_t = jnp.concatenate([cos, cos], axis=-1)
    sin_t = jnp.concatenate([-sin, sin], axis=-1)
    log_gamma = jnp.log(1.0 - 2.0 ** (-5.0 - jnp.arange(N_RET_HEADS, dtype=F32)))
    idx = jnp.arange(BLK, dtype=F32)
    dif = idx[:, None] - idx[None, :]
    dmat = jnp.where(dif[None] >= 0, jnp.exp(jnp.maximum(dif, 0.0)[None] * log_gamma[:, None, None]), 0.0)
    zeta = jnp.exp((BLK - 1.0 - idx)[None, :] * log_gamma[:, None])
    xi = jnp.exp((idx + 1.0)[None, :] * log_gamma[:, None])
    decay = jnp.exp(BLK * log_gamma)
    bshape = (N_RET_HEADS, BLK, RET_HEAD_DIM)
    zeta_b = jnp.broadcast_to(zeta[:, :, None], bshape)
    xi_b = jnp.broadcast_to(xi[:, :, None], bshape)
    return cos_t, sin_t, dmat, zeta_b, xi_b, decay


def kernel(x, ffn1_norm, ffn1_w_gate, ffn1_w_up, ffn1_w_down, mix_norm, w_in, attn_sinks, ret_gn_w, w_out,
           ffn2_norm, ffn2_w_gate, ffn2_w_up, ffn2_w_down, final_norm):
    cos_t, sin_t, dmat, zeta_b, xi_b, decay = _tables()
    pre_w = [w[0].astype(BF16) for w in (ffn1_w_gate, ffn1_w_up, ffn1_w_down, w_in)]
    pre_f32 = (ffn1_w_gate, ffn1_w_up, ffn1_w_down, w_in)
    post_f32 = (w_out, ffn2_w_gate, ffn2_w_up, ffn2_w_down)
    n1 = ffn1_norm.reshape(DEPTH, 1, D_MODEL)
    nm = mix_norm.reshape(DEPTH, 1, D_MODEL)
    n2 = ffn2_norm.reshape(DEPTH, 1, D_MODEL)
    gnw = ret_gn_w.reshape(DEPTH, 1, RET_WIDTH)
    fn = final_norm.reshape(1, D_MODEL)

    h = x.reshape(N_TOK, D_MODEL)
    for layer in range(DEPTH):
        final = layer == DEPTH - 1
        wg1, wu1, wd1, win = pre_w
        h, act, wo, wg2, wu2, wd2 = _pre_call(layer, h, n1, wg1, wu1, wd1, nm, win, cos_t, sin_t, post_f32)
        h, *pre_w = _mixpost_call(layer, final, attn_sinks[layer], decay, act, dmat, zeta_b, xi_b, gnw,
                                  h, wo, n2, wg2, wu2, wd2, fn, () if final else pre_f32)
    return h.reshape(BATCH, SEQ, D_MODEL)
```
